```python
import jax, jax.numpy as jnp
from jax import lax
import numpy as np

D_MODEL = 1024
BATCH = 32
SEQ = 2048
DEPTH = 2

CHUNK = 128
D_MIX = D_MODEL
GA_GROUPS = 4
GA_WIDTH = D_MIX // 4
GA_HEAD = GA_WIDTH // GA_GROUPS
MB_HEADS = 4
MB_WIDTH = 3 * D_MIX // 8
MB_HEAD = MB_WIDTH // MB_HEADS
MB_CONV = 4
SC_HEADS = 6
SC_WIDTH = D_MIX - GA_WIDTH - MB_WIDTH
SC_HEAD = SC_WIDTH // SC_HEADS
EPS = 1e-6
IN_SPLITS = (GA_WIDTH,) * 3 + (MB_WIDTH,) * 5 + (MB_HEADS,) * 2 + (SC_WIDTH,) * 4
D_IN = sum(IN_SPLITS)

kernel_name = "hybrid_gmlp_mlstm_stickbreaking_block"


def rms_norm(x, g):
    xf = x.astype(jnp.float32)
    y = xf * lax.rsqrt(jnp.mean(xf * xf, axis=-1, keepdims=True) + EPS)
    return (y * g.astype(jnp.float32)).astype(x.dtype)


def causal_conv(x, w, b):
    K, S = w.shape[0], x.shape[1]
    xp = jnp.pad(x, ((0, 0), (K - 1, 0), (0, 0)))
    y = b
    for tap in range(K):
        y = y + xp[:, tap:tap + S] * w[tap]
    return y


def spatial_gating_branch(u, v, v_norm_g, w_s, b_s):
    B, S, _ = u.shape
    nc = S // CHUNK
    u = jax.nn.gelu(u)
    v = rms_norm(jax.nn.gelu(v).reshape(B, nc, CHUNK, GA_GROUPS, GA_HEAD), v_norm_g)
    causal = jnp.tril(jnp.ones((CHUNK, CHUNK), dtype=bool))
    w = jnp.where(causal, w_s, 0).astype(v.dtype)
    sp = jnp.einsum('gts,bcsgd->bctgd', w, v) + b_s.T[:, :, None].astype(v.dtype)
    return u * sp.reshape(B, S, GA_WIDTH)


def mlstm_branch(q, k, v, i_raw, f_raw):
    out_dtype = v.dtype
    q, k, v = (a.astype(jnp.float32) for a in (q, k, v))
    B, S, H, d = q.shape
    nc = S // CHUNK
    k = k * (d ** -0.5)
    i_log = i_raw.astype(jnp.float32)
    f_log = jax.nn.log_sigmoid(f_raw.astype(jnp.float32))

    def to_chunks(a):
        a = a.reshape((B, nc, CHUNK, H) + a.shape[3:])
        return jnp.moveaxis(a, (1, 3), (0, 2))

    xs = (to_chunks(q), to_chunks(k), to_chunks(v), to_chunks(i_log), to_chunks(f_log))
    causal = jnp.tril(jnp.ones((CHUNK, CHUNK), dtype=bool))

    def step(carry, inp):
        C, n, m = carry
        qb, kb, vb, ib, lfb = inp
        bcum = jnp.cumsum(lfb, axis=-1)
        d_log = jnp.where(causal, bcum[..., :, None] - bcum[..., None, :] + ib[..., None, :], -jnp.inf)
        inter = bcum + m[..., None]
        m_t = jnp.maximum(inter, jnp.max(d_log, axis=-1))
        w_intra = jnp.exp(d_log - m_t[..., None])
        w_inter = jnp.exp(inter - m_t)
        s = jnp.einsum('bhtd,bhsd->bhts', qb, kb) * w_intra
        num = jnp.einsum('bhts,bhsd->bhtd', s, vb) + w_inter[..., None] * jnp.einsum('bhed,bhtd->bhte', C, qb)
        den = jnp.sum(s, axis=-1) + w_inter * jnp.einsum('bhd,bhtd->bht', n, qb)
        h = num / jnp.maximum(jnp.abs(den), jnp.exp(-m_t))[..., None]
        b_tot = bcum[..., -1]
        log_ws = b_tot[..., None] - bcum + ib
        m_new = jnp.maximum(b_tot + m, jnp.max(log_ws, axis=-1))
        decay = jnp.exp(b_tot + m - m_new)
        ws = jnp.exp(log_ws - m_new[..., None])
        C = decay[..., None, None] * C + jnp.einsum('bhs,bhse,bhsd->bhed', ws, vb, kb)
        n = decay[..., None] * n + jnp.einsum('bhs,bhsd->bhd', ws, kb)
        return (C, n, m_new), h

    init = (jnp.zeros((B, H, d, d), jnp.float32), jnp.zeros((B, H, d), jnp.float32),
            jnp.zeros((B, H), jnp.float32))
    _, hs = lax.scan(step, init, xs)
    h = jnp.moveaxis(hs, (0, 2), (1, 3)).reshape(B, S, H, d)
    return h.astype(out_dtype)


def stick_breaking_branch(q, k, v):
    B, S, H, d = q.shape
    scale = d ** -0.5
    outs = []
    for blk in range(S // CHUNK):
        t0 = blk * CHUNK
        kv_len = t0 + CHUNK
        qb, kb, vb = q[:, t0:kv_len], k[:, :kv_len], v[:, :kv_len]
        z = jnp.einsum('bthd,bshd->bhts', qb, kb).astype(jnp.float32) * scale
        t_idx = t0 + jnp.arange(CHUNK)
        s_idx = jnp.arange(kv_len)
        strict = s_idx[None, :] < t_idx[:, None]
        log_not = jnp.where(strict, jax.nn.log_sigmoid(-z), 0.0)
        suffix = lax.cumsum(log_not, axis=3, reverse=True) - log_not
        a = jnp.where(strict, jnp.exp(jax.nn.log_sigmoid(z) + suffix), 0.0)
        outs.append(jnp.einsum('bhts,bshd->bthd', a.astype(vb.dtype), vb))
    return jnp.concatenate(outs, axis=1)


def hybrid_layer(x, c_act, norm_g, w_ada, b_ada, w_in, ga_v_norm, ga_ws, ga_bs,
                 mb_conv_w, mb_conv_b, mb_b_i, mb_b_f, mb_h_norm, sc_q_norm, sc_k_norm, w_out):
    B, S, _ = x.shape
    shift, scale, gate = jnp.split(c_act @ w_ada + b_ada, 3, axis=-1)
    h = rms_norm(x, norm_g) * (1 + scale[:, None]) + shift[:, None]
    proj = h @ w_in
    offsets = [int(o) for o in np.cumsum(IN_SPLITS)[:-1]]
    (ga_u, ga_v, ga_z, mb_q, mb_k, mb_v, mb_o, mb_z, mb_i, mb_f,
     sc_q, sc_k, sc_v, sc_z) = jnp.split(proj, offsets, axis=-1)

    y_a = spatial_gating_branch(ga_u, ga_v, ga_v_norm, ga_ws, ga_bs) * jax.nn.silu(ga_z)

    qk = jax.nn.silu(causal_conv(jnp.concatenate([mb_q, mb_k], axis=-1), mb_conv_w, mb_conv_b))
    q_b, k_b = jnp.split(qk, 2, axis=-1)
    heads_b = lambda a: a.reshape(B, S, MB_HEADS, MB_HEAD)
    h_b = mlstm_branch(heads_b(q_b), heads_b(k_b), heads_b(mb_v), mb_i + mb_b_i, mb_f + mb_b_f)
    h_b = rms_norm(h_b, mb_h_norm).reshape(B, S, MB_WIDTH)
    y_b = jax.nn.sigmoid(mb_o) * h_b * jax.nn.silu(mb_z)

    heads_c = lambda a: a.reshape(B, S, SC_HEADS, SC_HEAD)
    h_c = stick_breaking_branch(rms_norm(heads_c(sc_q), sc_q_norm), rms_norm(heads_c(sc_k), sc_k_norm),
                                heads_c(sc_v))
    y_c = h_c.reshape(B, S, SC_WIDTH) * jax.nn.silu(sc_z)

    y = jnp.concatenate([y_a, y_b, y_c], axis=-1) @ w_out
    return x + gate[:, None] * y


def setup_inputs(seed: int = 0) -> dict:
    key = jax.random.key(seed)
    ks = jax.random.split(key, 17)
    nrm = lambda k, shape: jax.random.normal(k, shape, jnp.float32)
    D = D_MODEL
    return {
        "x": nrm(ks[0], (BATCH, SEQ, D)),
        "c": nrm(ks[1], (BATCH, D)),
        "norm_g": 1.0 + 0.02 * nrm(ks[2], (DEPTH, D)),
        "w_ada": nrm(ks[3], (DEPTH, D, 3 * D)) * (0.5 * D ** -0.5),
        "b_ada": 0.01 * nrm(ks[4], (DEPTH, 3 * D)),
        "w_in": nrm(ks[5], (DEPTH, D, D_IN)) * D ** -0.5,
        "ga_v_norm": 1.0 + 0.02 * nrm(ks[6], (DEPTH, GA_GROUPS, GA_HEAD)),
        "ga_ws": nrm(ks[7], (DEPTH, GA_GROUPS, CHUNK, CHUNK)) * CHUNK ** -0.5,
        "ga_bs": 1.0 + 0.02 * nrm(ks[8], (DEPTH, GA_GROUPS, CHUNK)),
        "mb_conv_w": nrm(ks[9], (DEPTH, MB_CONV, 2 * MB_WIDTH)) * MB_CONV ** -0.5,
        "mb_conv_b": 0.01 * nrm(ks[10], (DEPTH, 2 * MB_WIDTH)),
        "mb_b_i": 0.1 * nrm(ks[11], (DEPTH, MB_HEADS)),
        "mb_b_f": jnp.linspace(3.0, 6.0, MB_HEADS, dtype=jnp.float32)[None, :] + 0.01 * nrm(ks[12], (DEPTH, MB_HEADS)),
        "mb_h_norm": 1.0 + 0.02 * nrm(ks[13], (DEPTH, MB_HEADS, MB_HEAD)),
        "sc_q_norm": 1.0 + 0.02 * nrm(ks[14], (DEPTH, SC_HEAD)),
        "sc_k_norm": 1.0 + 0.02 * nrm(ks[15], (DEPTH, SC_HEAD)),
        "w_out": nrm(ks[16], (DEPTH, D_MIX, D)) * D_MIX ** -0.5,
    }


def reference(x, c, norm_g, w_ada, b_ada, w_in, ga_v_norm, ga_ws, ga_bs, mb_conv_w, mb_conv_b,
              mb_b_i, mb_b_f, mb_h_norm, sc_q_norm, sc_k_norm, w_out):
    c_act = jax.nn.silu(c)
    for l in range(DEPTH):
        x = hybrid_layer(x, c_act, norm_g[l], w_ada[l], b_ada[l], w_in[l], ga_v_norm[l], ga_ws[l],
                         ga_bs[l], mb_conv_w[l], mb_conv_b[l], mb_b_i[l], mb_b_f[l], mb_h_norm[l],
                         sc_q_norm[l], sc_k_norm[l], w_out[l])
    return x
```

```python
import functools

import jax
import jax.numpy as jnp
from jax import lax
from jax.experimental import pallas as pl
from jax.experimental.pallas import tpu as pltpu

CHUNK = 128
GA_GROUPS = 4
GA_HEAD = 64
GA_WIDTH = GA_GROUPS * GA_HEAD
MB_HEADS = 4
MB_HEAD = 96
MB_PAD = 128
MB_WIDTH = MB_HEADS * MB_HEAD
MB_PWIDTH = MB_HEADS * MB_PAD
MB_CONV = 4
SC_HEADS = 6
SC_HEAD = 64
SC_WIDTH = SC_HEADS * SC_HEAD
GATE_W = 128
EPS = 1e-6
LANE = 128
SC_TILE = 256
VMEM_LIMIT = 56 * 1024 * 1024

F32 = jnp.float32
BF16 = jnp.bfloat16
HIGHEST = lax.Precision.HIGHEST


def _dot(a, b):
    return jnp.dot(a, b, preferred_element_type=F32)


def _dot_nt(a, b):
    return lax.dot_general(a, b, (((1,), (1,)), ((), ())), preferred_element_type=F32)


def _split_dot(a, b):
    hi = a.astype(BF16)
    lo = (a - hi.astype(F32)).astype(BF16)
    return _dot(hi, b) + _dot(lo, b)


def _sigmoid(x):
    return 1.0 / (1.0 + jnp.exp(-x))


def _silu(x):
    return x * _sigmoid(x)


def _gelu_tanh(x):
    return 0.5 * x * (1.0 + jnp.tanh(0.7978845608028654 * (x + 0.044715 * (x * x * x))))


def _params(n_axes):
    return pltpu.CompilerParams(
        dimension_semantics=("arbitrary",) * n_axes, vmem_limit_bytes=VMEM_LIMIT)


def _ada_kernel(c_ref, w_ref, b_ref, o_ref):
    c = c_ref[...]
    o_ref[0] = jnp.dot(_silu(c), w_ref[0], precision=HIGHEST,
                       preferred_element_type=F32) + b_ref[0]


def _ada_call(c, w_ada, b_ada):
    depth, d, d3 = w_ada.shape
    b = c.shape[0]
    tn = 1024 if d3 % 1024 == 0 else d3
    return pl.pallas_call(
        _ada_kernel,
        grid=(depth, d3 // tn),
        in_specs=[
            pl.BlockSpec((b, d), lambda l, n: (0, 0)),
            pl.BlockSpec((1, d, tn), lambda l, n: (l, 0, n)),
            pl.BlockSpec((1, 1, tn), lambda l, n: (l, 0, n)),
        ],
        out_specs=pl.BlockSpec((1, b, tn), lambda l, n: (l, 0, n)),
        out_shape=jax.ShapeDtypeStruct((depth, b, d3), F32),
        compiler_params=_params(2),
        name="ada_mod",
    )(c, w_ada, b_ada.reshape(depth, 1, d3))


N_GA = 3 * GA_WIDTH
N_MB = 5 * MB_PWIDTH
N_SC = 4 * SC_WIDTH
OFF_MB = N_GA
OFF_GATE = OFF_MB + N_MB
OFF_SC = OFF_GATE + GATE_W
N_PROJ = OFF_SC + N_SC


def _inproj_kernel(x_ref, mod_ref, g_ref, w_ref, qg_ref, kg_ref, gm_ref,
                   ga_ref, mb_ref, gate_ref, sc_ref, *, d_model):
    x = x_ref[0]
    ms = jnp.mean(x * x, axis=-1, keepdims=True)
    y = x * lax.rsqrt(ms + EPS) * g_ref[...]
    shift = mod_ref[0, :, 0:d_model]
    scale = mod_ref[0, :, d_model:2 * d_model]
    hb = (y * (1.0 + scale) + shift).astype(BF16)

    def proj(c0, n):
        return _dot(hb, w_ref[:, c0:c0 + n])

    for c0 in range(0, N_GA, 256):
        ga_ref[0, :, c0:c0 + 256] = proj(c0, 256).astype(BF16)
    for c0 in range(0, N_MB, 512):
        mb_ref[0, :, c0:c0 + 512] = proj(OFF_MB + c0, 512).astype(BF16)
    gate_ref[0] = proj(OFF_GATE, GATE_W)

    def qk_norm(a, gain):
        msq = _split_dot(a * a, gm_ref[...])
        return (a * lax.rsqrt(msq + EPS) * gain).astype(BF16)

    sc_ref[0, :, 0:SC_WIDTH] = qk_norm(proj(OFF_SC, SC_WIDTH), qg_ref[...])
    sc_ref[0, :, SC_WIDTH:2 * SC_WIDTH] = qk_norm(proj(OFF_SC + SC_WIDTH, SC_WIDTH), kg_ref[...])
    for c0 in range(2 * SC_WIDTH, N_SC, SC_WIDTH):
        sc_ref[0, :, c0:c0 + SC_WIDTH] = proj(OFF_SC + c0, SC_WIDTH).astype(BF16)


def _inproj_call(x, mod3, norm_g, w_r, qg, kg, gmat):
    b, s, d = x.shape
    tm = 512 if s % 512 == 0 else s
    row = lambda bi, si: (bi, si, 0)
    const2 = lambda bi, si: (0, 0)
    return pl.pallas_call(
        functools.partial(_inproj_kernel, d_model=d),
        grid=(b, s // tm),
        in_specs=[
            pl.BlockSpec((1, tm, d), row),
            pl.BlockSpec((1, 1, 3 * d), lambda bi, si: (bi, 0, 0)),
            pl.BlockSpec((1, d), const2),
            pl.BlockSpec((d, N_PROJ), const2),
            pl.BlockSpec((1, SC_WIDTH), const2),
            pl.BlockSpec((1, SC_WIDTH), const2),
            pl.BlockSpec((SC_WIDTH, SC_WIDTH), const2),
        ],
        out_specs=[
            pl.BlockSpec((1, tm, N_GA), row),
            pl.BlockSpec((1, tm, N_MB), row),
            pl.BlockSpec((1, tm, GATE_W), row),
            pl.BlockSpec((1, tm, N_SC), row),
        ],
        out_shape=[
            jax.ShapeDtypeStruct((b, s, N_GA), BF16),
            jax.ShapeDtypeStruct((b, s, N_MB), BF16),
            jax.ShapeDtypeStruct((b, s, GATE_W), F32),
            jax.ShapeDtypeStruct((b, s, N_SC), BF16),
        ],
        compiler_params=_params(2),
        name="in_proj",
    )(x, mod3, norm_g, w_r, qg, kg, gmat)


def _ga_kernel(u_ref, v_ref, z_ref, vg_ref, ws_ref, bs_ref, gm_ref, y_ref):
    u = _gelu_tanh(u_ref[0].astype(F32))
    v = _gelu_tanh(v_ref[0].astype(F32))
    msq = _split_dot(v * v, gm_ref[...])
    vn = (v * lax.rsqrt(msq + EPS) * vg_ref[...]).astype(BF16)
    lane = lax.broadcasted_iota(jnp.int32, vn.shape, 1)
    t_idx = lax.broadcasted_iota(jnp.int32, (CHUNK, CHUNK), 0)
    s_idx = lax.broadcasted_iota(jnp.int32, (CHUNK, CHUNK), 1)
    sp = bs_ref[...]
    for g in range(GA_GROUPS):
        w = jnp.where(s_idx <= t_idx, ws_ref[g], 0.0).astype(BF16)
        in_group = (lane >= g * GA_HEAD) & (lane < (g + 1) * GA_HEAD)
        sp = sp + _dot(w, jnp.where(in_group, vn, jnp.zeros_like(vn)))
    y_ref[0] = (u * sp * _silu(z_ref[0].astype(F32))).astype(BF16)


def _ga_call(ga, vg, ws, bs_full, gmat):
    b, s, _ = ga.shape
    nc = s // CHUNK
    col = lambda j: (lambda bi, ci: (bi, ci, j))
    const2 = lambda bi, ci: (0, 0)
    return pl.pallas_call(
        _ga_kernel,
        grid=(b, nc),
        in_specs=[
            pl.BlockSpec((1, CHUNK, GA_WIDTH), col(0)),
            pl.BlockSpec((1, CHUNK, GA_WIDTH), col(1)),
            pl.BlockSpec((1, CHUNK, GA_WIDTH), col(2)),
            pl.BlockSpec((1, GA_WIDTH), const2),
            pl.BlockSpec((GA_GROUPS, CHUNK, CHUNK), lambda bi, ci: (0, 0, 0)),
            pl.BlockSpec((CHUNK, GA_WIDTH), const2),
            pl.BlockSpec((GA_WIDTH, GA_WIDTH), const2),
        ],
        out_specs=pl.BlockSpec((1, CHUNK, GA_WIDTH), col(0)),
        out_shape=jax.ShapeDtypeStruct((b, s, GA_WIDTH), BF16),
        compiler_params=_params(2),
        name="gmlp",
    )(ga, ga, ga, vg, ws, bs_full, gmat)


def _mb_kernel(q_ref, k_ref, v_ref, o_ref, z_ref, gate_ref, cw_ref, cb_ref, gb_ref,
               hn_ref, tri_ref, y_ref, c_state, m_state, ext_ref):
    ci = pl.program_id(1)

    @pl.when(ci == 0)
    def _():
        c_state[...] = jnp.zeros_like(c_state)
        m_state[...] = jnp.zeros_like(m_state)
        ext_ref[0:8, :] = jnp.zeros((8, 2 * MB_PWIDTH), F32)

    qk_raw = jnp.concatenate([q_ref[0], k_ref[0]], axis=1).astype(F32)
    ext_ref[8:8 + CHUNK, :] = qk_raw
    conv = cb_ref[...] + cw_ref[MB_CONV - 1:MB_CONV, :] * qk_raw
    for back in range(1, MB_CONV):
        tap = MB_CONV - 1 - back
        conv = conv + cw_ref[tap:tap + 1, :] * ext_ref[8 - back:8 - back + CHUNK, :]
    ext_ref[0:8, :] = qk_raw[CHUNK - 8:CHUNK, :]
    qk = _silu(conv)
    q_all = qk[:, 0:MB_PWIDTH].astype(BF16)
    k_all = (qk[:, MB_PWIDTH:2 * MB_PWIDTH] * (MB_HEAD ** -0.5)).astype(BF16)

    graw = gate_ref[0] + gb_ref[...]
    lane_g = lax.broadcasted_iota(jnp.int32, graw.shape, 1)
    log_f = jnp.minimum(graw, 0.0) - jnp.log1p(jnp.exp(-jnp.abs(graw)))
    gm = jnp.where((lane_g >= MB_HEADS) & (lane_g < 2 * MB_HEADS), log_f, graw)
    bc = jnp.dot(tri_ref[...], gm, precision=HIGHEST, preferred_element_type=F32)
    gm_t = gm.T
    bc_t = bc.T

    t_idx = lax.broadcasted_iota(jnp.int32, (CHUNK, CHUNK), 0)
    s_idx = lax.broadcasted_iota(jnp.int32, (CHUNK, CHUNK), 1)
    causal = s_idx <= t_idx
    lane = lax.broadcasted_iota(jnp.int32, (CHUNK, MB_PAD), 1)

    for h in range(MB_HEADS):
        cs = slice(h * MB_PAD, (h + 1) * MB_PAD)
        icol = gm[:, h:h + 1]
        irow = gm_t[h:h + 1, :]
        bcol = bc[:, MB_HEADS + h:MB_HEADS + h + 1]
        brow = bc_t[MB_HEADS + h:MB_HEADS + h + 1, :]
        m_prev = m_state[h:h + 1, 0:1]

        d_log = jnp.where(causal, bcol - brow + irow, -jnp.inf)
        inter = bcol + m_prev
        m_t = jnp.maximum(inter, jnp.max(d_log, axis=1, keepdims=True))
        w_intra = jnp.exp(d_log - m_t)
        w_inter = jnp.exp(inter - m_t)

        qh = q_all[:, cs]
        kh = k_all[:, cs]
        v_aug = jnp.where(lane == MB_HEAD, 1.0, v_ref[0, :, cs].astype(F32))
        s_mat = _dot_nt(qh, kh) * w_intra
        num = _dot(s_mat.astype(BF16), v_aug.astype(BF16)) \
            + w_inter * _dot_nt(qh, c_state[h].astype(BF16))
        den = num[:, MB_HEAD:MB_HEAD + 1]
        hh = num / jnp.maximum(jnp.abs(den), jnp.exp(-m_t))
        hh = jnp.where(lane < MB_HEAD, hh, 0.0)

        b_tot = bcol[CHUNK - 1:CHUNK, :]
        m_new = jnp.maximum(b_tot + m_prev,
                            jnp.max(b_tot - brow + irow, axis=1, keepdims=True))
        decay = jnp.exp(b_tot + m_prev - m_new)
        ws_col = jnp.exp(b_tot - bcol + icol - m_new)
        vw_t = (v_aug * ws_col).T.astype(BF16)
        c_state[h] = decay * c_state[h] + _dot(vw_t, kh)
        m_state[h:h + 1, :] = jnp.broadcast_to(m_new, (1, LANE))

        msq = jnp.sum(hh * hh, axis=1, keepdims=True) * (1.0 / MB_HEAD)
        hn = hh * lax.rsqrt(msq + EPS) * hn_ref[:, cs]
        y = _sigmoid(o_ref[0, :, cs].astype(F32)) * hn * _silu(z_ref[0, :, cs].astype(F32))
        y_ref[0, :, cs] = y.astype(BF16)


def _mb_call(mb, gates, conv_w, conv_b, gate_b, hnorm, tri):
    b, s, _ = mb.shape
    nc = s // CHUNK
    col = lambda j: (lambda bi, ci: (bi, ci, j))
    const2 = lambda bi, ci: (0, 0)
    return pl.pallas_call(
        _mb_kernel,
        grid=(b, nc),
        in_specs=[
            pl.BlockSpec((1, CHUNK, MB_PWIDTH), col(0)),
            pl.BlockSpec((1, CHUNK, MB_PWIDTH), col(1)),
            pl.BlockSpec((1, CHUNK, MB_PWIDTH), col(2)),
            pl.BlockSpec((1, CHUNK, MB_PWIDTH), col(3)),
            pl.BlockSpec((1, CHUNK, MB_PWIDTH), col(4)),
            pl.BlockSpec((1, CHUNK, GATE_W), col(0)),
            pl.BlockSpec((MB_CONV, 2 * MB_PWIDTH), const2),
            pl.BlockSpec((1, 2 * MB_PWIDTH), const2),
            pl.BlockSpec((1, GATE_W), const2),
            pl.BlockSpec((1, MB_PWIDTH), const2),
            pl.BlockSpec((CHUNK, CHUNK), const2),
        ],
        out_specs=pl.BlockSpec((1, CHUNK, MB_PWIDTH), col(0)),
        out_shape=jax.ShapeDtypeStruct((b, s, MB_PWIDTH), BF16),
        scratch_shapes=[
            pltpu.VMEM((MB_HEADS, MB_PAD, MB_PAD), F32),
            pltpu.VMEM((8, LANE), F32),
            pltpu.VMEM((8 + CHUNK, 2 * MB_PWIDTH), F32),
        ],
        compiler_params=_params(2),
        name="mlstm",
    )(mb, mb, mb, mb, mb, gates, conv_w, conv_b, gate_b, hnorm, tri)


def _sc_kernel(q_ref, k_ref, v_ref, z_ref, tri_ref, y_ref, acc_ref, o_ref, *, tile):
    qi = pl.program_id(2)
    q = q_ref[0]
    lane_q = lax.broadcasted_iota(jnp.int32, q.shape, 1)
    zero_q = jnp.zeros_like(q)
    qs = jnp.concatenate([jnp.where(lane_q < SC_HEAD, q, zero_q),
                          jnp.where(lane_q >= SC_HEAD, q, zero_q)], axis=0)
    acc_ref[...] = jnp.zeros_like(acc_ref)
    o_ref[...] = jnp.zeros_like(o_ref)

    def key_tile(j, diag):
        start = pl.multiple_of(j * tile, tile)
        kt = k_ref[0, pl.ds(start, tile), :]
        vt = v_ref[0, pl.ds(start, tile), :]
        z = _dot_nt(qs, kt)
        sp = jnp.maximum(z, 0.0) + jnp.log(1.0 + jnp.exp(-jnp.abs(z)))
        if diag:
            t_idx = lax.broadcasted_iota(jnp.int32, (tile, tile), 0)
            s_idx = lax.broadcasted_iota(jnp.int32, (tile, tile), 1)
            strict = jnp.concatenate([s_idx < t_idx, s_idx < t_idx], axis=0)
            sp = jnp.where(strict, sp, 0.0)
        cs = _split_dot(sp, tri_ref[...])
        acc = acc_ref[...]
        w = jnp.exp(z - cs - jnp.concatenate([acc] * (tile // LANE), axis=1))
        if diag:
            w = jnp.where(strict, w, 0.0)
        acc_ref[...] = acc + jnp.broadcast_to(cs[:, 0:1], acc.shape)
        wb = w.astype(BF16)
        lane_v = lax.broadcasted_iota(jnp.int32, vt.shape, 1)
        zero_v = jnp.zeros_like(vt)
        o_ref[...] += _dot(wb[0:tile], jnp.where(lane_v < SC_HEAD, vt, zero_v)) \
            + _dot(wb[tile:2 * tile], jnp.where(lane_v >= SC_HEAD, vt, zero_v))

    key_tile(qi, True)

    def body(it, carry):
        key_tile(qi - 1 - it, False)
        return carry

    lax.fori_loop(0, qi, body, 0)
    y_ref[0] = (o_ref[...] * _silu(z_ref[0].astype(F32))).astype(BF16)


def _sc_call(sc, tri):
    b, s, _ = sc.shape
    tile = SC_TILE
    nq = s // tile
    npair = SC_WIDTH // LANE
    return pl.pallas_call(
        functools.partial(_sc_kernel, tile=tile),
        grid=(b, npair, nq),
        in_specs=[
            pl.BlockSpec((1, tile, LANE), lambda bi, p, qi: (bi, qi, p)),
            pl.BlockSpec((1, s, LANE), lambda bi, p, qi: (bi, 0, npair + p)),
            pl.BlockSpec((1, s, LANE), lambda bi, p, qi: (bi, 0, 2 * npair + p)),
            pl.BlockSpec((1, tile, LANE), lambda bi, p, qi: (bi, qi, 3 * npair + p)),
            pl.BlockSpec((tile, tile), lambda bi, p, qi: (0, 0)),
        ],
        out_specs=pl.BlockSpec((1, tile, LANE), lambda bi, p, qi: (bi, qi, p)),
        out_shape=jax.ShapeDtypeStruct((b, s, SC_WIDTH), BF16),
        scratch_shapes=[
            pltpu.VMEM((2 * tile, LANE), F32),
            pltpu.VMEM((tile, LANE), F32),
        ],
        compiler_params=_params(3),
        name="stick_breaking",
    )(sc, sc, sc, sc, tri)


def _out_kernel(x_ref, mod_ref, ya_ref, yb_ref, yc_ref, wa_ref, wb_ref, wc_ref, o_ref,
                *, d_model):
    y = _dot(ya_ref[0], wa_ref[...]) + _dot(yb_ref[0], wb_ref[...]) + _dot(yc_ref[0], wc_ref[...])
    gate = mod_ref[0, :, 2 * d_model:3 * d_model]
    o_ref[0] = x_ref[0] + gate * y


def _out_call(x, mod3, ya, yb, yc, wa, wb, wc):
    b, s, d = x.shape
    tm = 512 if s % 512 == 0 else s
    row = lambda bi, si: (bi, si, 0)
    const2 = lambda bi, si: (0, 0)
    return pl.pallas_call(
        functools.partial(_out_kernel, d_model=d),
        grid=(b, s // tm),
        in_specs=[
            pl.BlockSpec((1, tm, d), row),
            pl.BlockSpec((1, 1, 3 * d), lambda bi, si: (bi, 0, 0)),
            pl.BlockSpec((1, tm, GA_WIDTH), row),
            pl.BlockSpec((1, tm, MB_PWIDTH), row),
            pl.BlockSpec((1, tm, SC_WIDTH), row),
            pl.BlockSpec((GA_WIDTH, d), const2),
            pl.BlockSpec((MB_PWIDTH, d), const2),
            pl.BlockSpec((SC_WIDTH, d), const2),
        ],
        out_specs=pl.BlockSpec((1, tm, d), row),
        out_shape=jax.ShapeDtypeStruct((b, s, d), F32),
        compiler_params=_params(2),
        name="out_proj",
    )(x, mod3, ya, yb, yc, wa, wb, wc)


def _pad_heads(a):
    lead = a.shape[:-1]
    a = a.reshape(lead + (MB_HEADS, MB_HEAD))
    a = jnp.pad(a, [(0, 0)] * len(lead) + [(0, 0), (0, MB_PAD - MB_HEAD)])
    return a.reshape(lead + (MB_PWIDTH,))


def _block_mean_matrix(width, group):
    idx = jnp.arange(width) // group
    return jnp.where(idx[:, None] == idx[None, :], 1.0 / group, 0.0).astype(BF16)


def _layer_params(w_in, ga_v_norm, ga_bs, mb_conv_w, mb_conv_b, mb_b_i, mb_b_f, mb_h_norm,
                  sc_q_norm, sc_k_norm, w_out):
    d = w_in.shape[0]
    o_mb = N_GA
    o_gate = o_mb + 5 * MB_WIDTH
    o_sc = o_gate + 2 * MB_HEADS
    w_mb = jnp.concatenate(
        [_pad_heads(w_in[:, o_mb + i * MB_WIDTH:o_mb + (i + 1) * MB_WIDTH]) for i in range(5)], axis=1)
    w_gate = jnp.pad(w_in[:, o_gate:o_sc], ((0, 0), (0, GATE_W - 2 * MB_HEADS)))
    w_r = jnp.concatenate([w_in[:, :N_GA], w_mb, w_gate, w_in[:, o_sc:]], axis=1).astype(BF16)

    conv_w = jnp.concatenate([_pad_heads(mb_conv_w[:, :MB_WIDTH]), _pad_heads(mb_conv_w[:, MB_WIDTH:])], axis=1)
    conv_b = jnp.concatenate([_pad_heads(mb_conv_b[:MB_WIDTH]), _pad_heads(mb_conv_b[MB_WIDTH:])])[None, :]
    gate_b = jnp.pad(jnp.concatenate([mb_b_i, mb_b_f]), (0, GATE_W - 2 * MB_HEADS))[None, :]
    hnorm = _pad_heads(mb_h_norm.reshape(MB_WIDTH))[None, :]

    qg = (jnp.tile(sc_q_norm, SC_HEADS) * (SC_HEAD ** -0.5))[None, :]
    kg = jnp.tile(sc_k_norm, SC_HEADS)[None, :]
    vg = ga_v_norm.reshape(1, GA_WIDTH)
    bs_full = jnp.repeat(ga_bs.T, GA_HEAD, axis=1)

    wa = w_out[:GA_WIDTH].astype(BF16)
    wb = w_out[GA_WIDTH:GA_WIDTH + MB_WIDTH].reshape(MB_HEADS, MB_HEAD, d)
    wb = jnp.pad(wb, ((0, 0), (0, MB_PAD - MB_HEAD), (0, 0))).reshape(MB_PWIDTH, d).astype(BF16)
    wc = w_out[GA_WIDTH + MB_WIDTH:].astype(BF16)
    return dict(w_r=w_r, conv_w=conv_w, conv_b=conv_b, gate_b=gate_b, hnorm=hnorm, qg=qg, kg=kg,
                vg=vg, bs_full=bs_full, wa=wa, wb=wb, wc=wc)


def kernel(x, c, norm_g, w_ada, b_ada, w_in, ga_v_norm, ga_ws, ga_bs, mb_conv_w, mb_conv_b,
           mb_b_i, mb_b_f, mb_h_norm, sc_q_norm, sc_k_norm, w_out):
    depth = w_in.shape[0]
    b, s, d = x.shape
    assert s % SC_TILE == 0 and s % CHUNK == 0 and d % LANE == 0

    mod = _ada_call(c, w_ada, b_ada)
    gm_sc = _block_mean_matrix(SC_WIDTH, SC_HEAD)
    gm_ga = _block_mean_matrix(GA_WIDTH, GA_HEAD)
    r = jnp.arange(CHUNK)
    tri_low = (r[None, :] <= r[:, None]).astype(F32)
    r2 = jnp.arange(SC_TILE)
    tri_suffix = (r2[:, None] >= r2[None, :]).astype(BF16)

    for l in range(depth):
        p = _layer_params(w_in[l], ga_v_norm[l], ga_bs[l], mb_conv_w[l], mb_conv_b[l], mb_b_i[l],
                          mb_b_f[l], mb_h_norm[l], sc_q_norm[l], sc_k_norm[l], w_out[l])
        mod3 = mod[l][:, None, :]
        ga, mb, gates, sc = _inproj_call(x, mod3, norm_g[l][None, :], p["w_r"], p["qg"], p["kg"], gm_sc)
        ya = _ga_call(ga, p["vg"], ga_ws[l], p["bs_full"], gm_ga)
        yb = _mb_call(mb, gates, p["conv_w"], p["conv_b"], p["gate_b"], p["hnorm"], tri_low)
        yc = _sc_call(sc, tri_suffix)
        x = _out_call(x, mod3, ya, yb, yc, p["wa"], p["wb"], p["wc"])
    return x
```

```python
import functools

import jax
import jax.numpy as jnp
from jax import lax
from jax.experimental import pallas as pl
from jax.experimental.pallas import tpu as pltpu

CHUNK = 128
GA_GROUPS = 4
GA_HEAD = 64
GA_WIDTH = GA_GROUPS * GA_HEAD
MB_HEADS = 4
MB_HEAD = 96
MB_PAD = 128
MB_WIDTH = MB_HEADS * MB_HEAD
MB_PWIDTH = MB_HEADS * MB_PAD
MB_CONV = 4
SC_HEADS = 6
SC_HEAD = 64
SC_WIDTH = SC_HEADS * SC_HEAD
GATE_W = 128
EPS = 1e-6
LANE = 128
SC_TILE = 256
VMEM_LIMIT = 56 * 1024 * 1024

F32 = jnp.float32
BF16 = jnp.bfloat16
HIGHEST = lax.Precision.HIGHEST


def _dot(a, b):
    return jnp.dot(a, b, preferred_element_type=F32)


def _dot_nt(a, b):
    return lax.dot_general(a, b, (((1,), (1,)), ((), ())), preferred_element_type=F32)


def _split_dot(a, b2):
    hi = a.astype(BF16)
    lo = (a - hi.astype(F32)).astype(BF16)
    return _dot(jnp.concatenate([hi, lo], axis=1), b2)


def _sigmoid(x):
    return 1.0 / (1.0 + jnp.exp(-x))


def _silu(x):
    return x * _sigmoid(x)


def _gelu_tanh(x):
    return 0.5 * x * (1.0 + jnp.tanh(0.7978845608028654 * (x + 0.044715 * (x * x * x))))


def _params(n_axes):
    return pltpu.CompilerParams(
        dimension_semantics=("arbitrary",) * n_axes, vmem_limit_bytes=VMEM_LIMIT)


def _ada_kernel(c_ref, w_ref, b_ref, o_ref):
    c = c_ref[...]
    o_ref[0] = jnp.dot(_silu(c), w_ref[0], precision=HIGHEST,
                       preferred_element_type=F32) + b_ref[0]


def _ada_call(c, w_ada, b_ada):
    depth, d, d3 = w_ada.shape
    b = c.shape[0]
    tn = 1024 if d3 % 1024 == 0 else d3
    return pl.pallas_call(
        _ada_kernel,
        grid=(depth, d3 // tn),
        in_specs=[
            pl.BlockSpec((b, d), lambda l, n: (0, 0)),
            pl.BlockSpec((1, d, tn), lambda l, n: (l, 0, n)),
            pl.BlockSpec((1, 1, tn), lambda l, n: (l, 0, n)),
        ],
        out_specs=pl.BlockSpec((1, b, tn), lambda l, n: (l, 0, n)),
        out_shape=jax.ShapeDtypeStruct((depth, b, d3), F32),
        compiler_params=_params(2),
        name="ada_mod",
    )(c, w_ada, b_ada.reshape(depth, 1, d3))


N_GA = 3 * GA_WIDTH
N_MB = 5 * MB_PWIDTH
N_SC = 4 * SC_WIDTH
OFF_MB = N_GA
OFF_GATE = OFF_MB + N_MB
OFF_SC = OFF_GATE + GATE_W
N_PROJ = OFF_SC + N_SC


def _inproj_kernel(x_ref, mod_ref, g_ref, w_ref, qkg_ref, gm_ref,
                   ga_ref, mb_ref, gate_ref, sc_ref, *, d_model):
    x = x_ref[0]
    ms = jnp.mean(x * x, axis=-1, keepdims=True)
    y = x * lax.rsqrt(ms + EPS) * g_ref[...]
    shift = mod_ref[0, :, 0:d_model]
    scale = mod_ref[0, :, d_model:2 * d_model]
    hb = (y * (1.0 + scale) + shift).astype(BF16)

    def proj(c0, n):
        return _dot(hb, w_ref[:, c0:c0 + n])

    for c0 in range(0, N_GA, 256):
        ga_ref[0, :, c0:c0 + 256] = proj(c0, 256).astype(BF16)
    for c0 in range(0, N_MB, 512):
        mb_ref[0, :, c0:c0 + 512] = proj(OFF_MB + c0, 512).astype(BF16)
    gate_ref[0] = proj(OFF_GATE, GATE_W)

    qk = proj(OFF_SC, 2 * SC_WIDTH)
    msq = _dot((qk * qk).astype(BF16), gm_ref[...])
    sc_ref[0, :, 0:2 * SC_WIDTH] = (qk * lax.rsqrt(msq + EPS) * qkg_ref[...]).astype(BF16)
    for c0 in range(2 * SC_WIDTH, N_SC, SC_WIDTH):
        sc_ref[0, :, c0:c0 + SC_WIDTH] = proj(OFF_SC + c0, SC_WIDTH).astype(BF16)


def _inproj_call(x, mod3, norm_g, w_r, qkg, gmat):
    b, s, d = x.shape
    tm = 512 if s % 512 == 0 else s
    row = lambda bi, si: (bi, si, 0)
    const2 = lambda bi, si: (0, 0)
    return pl.pallas_call(
        functools.partial(_inproj_kernel, d_model=d),
        grid=(b, s // tm),
        in_specs=[
            pl.BlockSpec((1, tm, d), row),
            pl.BlockSpec((1, 1, 3 * d), lambda bi, si: (bi, 0, 0)),
            pl.BlockSpec((1, d), const2),
            pl.BlockSpec((d, N_PROJ), const2),
            pl.BlockSpec((1, 2 * SC_WIDTH), const2),
            pl.BlockSpec((2 * SC_WIDTH, 2 * SC_WIDTH), const2),
        ],
        out_specs=[
            pl.BlockSpec((1, tm, N_GA), row),
            pl.BlockSpec((1, tm, N_MB), row),
            pl.BlockSpec((1, tm, GATE_W), row),
            pl.BlockSpec((1, tm, N_SC), row),
        ],
        out_shape=[
            jax.ShapeDtypeStruct((b, s, N_GA), BF16),
            jax.ShapeDtypeStruct((b, s, N_MB), BF16),
            jax.ShapeDtypeStruct((b, s, GATE_W), F32),
            jax.ShapeDtypeStruct((b, s, N_SC), BF16),
        ],
        compiler_params=_params(2),
        name="in_proj",
    )(x, mod3, norm_g, w_r, qkg, gmat)


GA_CHUNKS = 4


def _ga_kernel(u_ref, v_ref, z_ref, vg_ref, ws_ref, bs_ref, gm_ref, y_ref, *, nchunk):
    t_idx = lax.broadcasted_iota(jnp.int32, (CHUNK, CHUNK), 0)
    s_idx = lax.broadcasted_iota(jnp.int32, (CHUNK, CHUNK), 1)
    lane = lax.broadcasted_iota(jnp.int32, (CHUNK, GA_WIDTH), 1)
    w_causal = [jnp.where(s_idx <= t_idx, ws_ref[g], 0.0).astype(BF16) for g in range(GA_GROUPS)]
    for c in range(nchunk):
        rows = slice(c * CHUNK, (c + 1) * CHUNK)
        u = _gelu_tanh(u_ref[0, rows, :].astype(F32))
        v = _gelu_tanh(v_ref[0, rows, :].astype(F32))
        msq = _split_dot(v * v, gm_ref[...])
        vn = (v * lax.rsqrt(msq + EPS) * vg_ref[...]).astype(BF16)
        sp = bs_ref[...]
        for g in range(GA_GROUPS):
            in_group = (lane >= g * GA_HEAD) & (lane < (g + 1) * GA_HEAD)
            sp = sp + _dot(w_causal[g], jnp.where(in_group, vn, jnp.zeros_like(vn)))
        y_ref[0, rows, :] = (u * sp * _silu(z_ref[0, rows, :].astype(F32))).astype(BF16)


def _ga_call(ga, vg, ws, bs_full, gmat):
    b, s, _ = ga.shape
    nchunk = GA_CHUNKS if s % (GA_CHUNKS * CHUNK) == 0 else 1
    rows = nchunk * CHUNK
    col = lambda j: (lambda bi, ci: (bi, ci, j))
    const2 = lambda bi, ci: (0, 0)
    return pl.pallas_call(
        functools.partial(_ga_kernel, nchunk=nchunk),
        grid=(b, s // rows),
        in_specs=[
            pl.BlockSpec((1, rows, GA_WIDTH), col(0)),
            pl.BlockSpec((1, rows, GA_WIDTH), col(1)),
            pl.BlockSpec((1, rows, GA_WIDTH), col(2)),
            pl.BlockSpec((1, GA_WIDTH), const2),
            pl.BlockSpec((GA_GROUPS, CHUNK, CHUNK), lambda bi, ci: (0, 0, 0)),
            pl.BlockSpec((CHUNK, GA_WIDTH), const2),
            pl.BlockSpec((2 * GA_WIDTH, GA_WIDTH), const2),
        ],
        out_specs=pl.BlockSpec((1, rows, GA_WIDTH), col(0)),
        out_shape=jax.ShapeDtypeStruct((b, s, GA_WIDTH), BF16),
        compiler_params=_params(2),
        name="gmlp",
    )(ga, ga, ga, vg, ws, bs_full, gmat)


def _mb_kernel(q_ref, k_ref, v_ref, o_ref, z_ref, gate_ref, cw_ref, cb_ref, gb_ref,
               hn_ref, tri_ref, y_ref, c_state, m_state, ext_ref):
    ci = pl.program_id(1)

    @pl.when(ci == 0)
    def _():
        c_state[...] = jnp.zeros_like(c_state)
        m_state[...] = jnp.zeros_like(m_state)
        ext_ref[0:8, :] = jnp.zeros((8, 2 * MB_PWIDTH), F32)

    qk_raw = jnp.concatenate([q_ref[0], k_ref[0]], axis=1).astype(F32)
    ext_ref[8:8 + CHUNK, :] = qk_raw
    conv = cb_ref[...] + cw_ref[MB_CONV - 1:MB_CONV, :] * qk_raw
    for back in range(1, MB_CONV):
        tap = MB_CONV - 1 - back
        conv = conv + cw_ref[tap:tap + 1, :] * ext_ref[8 - back:8 - back + CHUNK, :]
    ext_ref[0:8, :] = qk_raw[CHUNK - 8:CHUNK, :]
    qk = _silu(conv)
    q_all = qk[:, 0:MB_PWIDTH].astype(BF16)
    k_all = (qk[:, MB_PWIDTH:2 * MB_PWIDTH] * (MB_HEAD ** -0.5)).astype(BF16)

    graw = gate_ref[0] + gb_ref[...]
    lane_g = lax.broadcasted_iota(jnp.int32, graw.shape, 1)
    log_f = jnp.minimum(graw, 0.0) - jnp.log1p(jnp.exp(-jnp.abs(graw)))
    gm = jnp.where((lane_g >= MB_HEADS) & (lane_g < 2 * MB_HEADS), log_f, graw)
    bc = jnp.dot(tri_ref[...], gm, precision=HIGHEST, preferred_element_type=F32)
    gm_t = gm.T
    bc_t = bc.T

    t_idx = lax.broadcasted_iota(jnp.int32, (CHUNK, CHUNK), 0)
    s_idx = lax.broadcasted_iota(jnp.int32, (CHUNK, CHUNK), 1)
    causal = s_idx <= t_idx
    lane = lax.broadcasted_iota(jnp.int32, (CHUNK, MB_PAD), 1)

    for h in range(MB_HEADS):
        cs = slice(h * MB_PAD, (h + 1) * MB_PAD)
        icol = gm[:, h:h + 1]
        irow = gm_t[h:h + 1, :]
        bcol = bc[:, MB_HEADS + h:MB_HEADS + h + 1]
        brow = bc_t[MB_HEADS + h:MB_HEADS + h + 1, :]
        m_prev = m_state[h:h + 1, 0:1]

        d_log = jnp.where(causal, bcol - brow + irow, -jnp.inf)
        inter = bcol + m_prev
        m_t = jnp.maximum(inter, jnp.max(d_log, axis=1, keepdims=True))
        w_intra = jnp.exp(d_log - m_t)
        w_inter = jnp.exp(inter - m_t)

        qh = q_all[:, cs]
        kh = k_all[:, cs]
        v_aug = jnp.where(lane == MB_HEAD, 1.0, v_ref[0, :, cs].astype(F32))
        s_mat = _dot_nt(qh, kh) * w_intra
        num = _dot(s_mat.astype(BF16), v_aug.astype(BF16)) \
            + w_inter * _dot_nt(qh, c_state[h].astype(BF16))
        den = num[:, MB_HEAD:MB_HEAD + 1]
        hh = num / jnp.maximum(jnp.abs(den), jnp.exp(-m_t))
        hh = jnp.where(lane < MB_HEAD, hh, 0.0)

        b_tot = bcol[CHUNK - 1:CHUNK, :]
        m_new = jnp.maximum(b_tot + m_prev,
                            jnp.max(b_tot - brow + irow, axis=1, keepdims=True))
        decay = jnp.exp(b_tot + m_prev - m_new)
        ws_col = jnp.exp(b_tot - bcol + icol - m_new)
        vw_t = (v_aug * ws_col).T.astype(BF16)
        c_state[h] = decay * c_state[h] + _dot(vw_t, kh)
        m_state[h:h + 1, :] = jnp.broadcast_to(m_new, (1, LANE))

        msq = jnp.sum(hh * hh, axis=1, keepdims=True) * (1.0 / MB_HEAD)
        hn = hh * lax.rsqrt(msq + EPS) * hn_ref[:, cs]
        y = _sigmoid(o_ref[0, :, cs].astype(F32)) * hn * _silu(z_ref[0, :, cs].astype(F32))
        y_ref[0, :, cs] = y.astype(BF16)


def _mb_call(mb, gates, conv_w, conv_b, gate_b, hnorm, tri):
    b, s, _ = mb.shape
    nc = s // CHUNK
    col = lambda j: (lambda bi, ci: (bi, ci, j))
    const2 = lambda bi, ci: (0, 0)
    return pl.pallas_call(
        _mb_kernel,
        grid=(b, nc),
        in_specs=[
            pl.BlockSpec((1, CHUNK, MB_PWIDTH), col(0)),
            pl.BlockSpec((1, CHUNK, MB_PWIDTH), col(1)),
            pl.BlockSpec((1, CHUNK, MB_PWIDTH), col(2)),
            pl.BlockSpec((1, CHUNK, MB_PWIDTH), col(3)),
            pl.BlockSpec((1, CHUNK, MB_PWIDTH), col(4)),
            pl.BlockSpec((1, CHUNK, GATE_W), col(0)),
            pl.BlockSpec((MB_CONV, 2 * MB_PWIDTH), const2),
            pl.BlockSpec((1, 2 * MB_PWIDTH), const2),
            pl.BlockSpec((1, GATE_W), const2),
            pl.BlockSpec((1, MB_PWIDTH), const2),
            pl.BlockSpec((CHUNK, CHUNK), const2),
        ],
        out_specs=pl.BlockSpec((1, CHUNK, MB_PWIDTH), col(0)),
        out_shape=jax.ShapeDtypeStruct((b, s, MB_PWIDTH), BF16),
        scratch_shapes=[
            pltpu.VMEM((MB_HEADS, MB_PAD, MB_PAD), F32),
            pltpu.VMEM((8, LANE), F32),
            pltpu.VMEM((8 + CHUNK, 2 * MB_PWIDTH), F32),
        ],
        compiler_params=_params(2),
        name="mlstm",
    )(mb, mb, mb, mb, mb, gates, conv_w, conv_b, gate_b, hnorm, tri)


LOG2E = 1.4426950408889634


def _neg_abs(z):
    bits = lax.bitcast_convert_type(z, jnp.uint32) | jnp.uint32(0x80000000)
    return lax.bitcast_convert_type(bits, F32)


def _sc_kernel(q_ref, k_ref, v_ref, z_ref, tri_ref, y_ref, qm_ref, acc_ref, o_ref, *, tile):
    qi = pl.program_id(1)
    npair = SC_WIDTH // LANE
    low = lax.broadcasted_iota(jnp.int32, (tile, LANE), 1) < SC_HEAD
    for p in range(npair):
        q = q_ref[0, :, p * LANE:(p + 1) * LANE]
        zero = jnp.zeros_like(q)
        qm_ref[2 * p] = jnp.where(low, q, zero)
        qm_ref[2 * p + 1] = jnp.where(low, zero, q)
    acc_ref[...] = jnp.zeros_like(acc_ref)
    o_ref[...] = jnp.zeros_like(o_ref)

    def key_tile(j, diag):
        start = pl.multiple_of(j * tile, tile)
        if diag:
            t_idx = lax.broadcasted_iota(jnp.int32, (tile, tile), 0)
            s_idx = lax.broadcasted_iota(jnp.int32, (tile, tile), 1)
            strict = s_idx < t_idx
        for p in range(npair):
            cols = slice(p * LANE, (p + 1) * LANE)
            kt = k_ref[0, pl.ds(start, tile), cols]
            vt = v_ref[0, pl.ds(start, tile), cols]
            zero_v = jnp.zeros_like(vt)
            vm = (jnp.where(low, vt, zero_v), jnp.where(low, zero_v, vt))
            o_pair = None
            for hh in range(2):
                h = 2 * p + hh
                z = _dot_nt(qm_ref[h], kt)
                sp = jnp.maximum(z, 0.0) + jnp.log(1.0 + jnp.exp2(_neg_abs(z))) * LOG2E
                if diag:
                    sp = jnp.where(strict, sp, 0.0)
                cs = _split_dot(sp, tri_ref[...])
                acc = acc_ref[h]
                w = jnp.exp2(z - cs - jnp.concatenate([acc] * (tile // LANE), axis=1))
                if diag:
                    w = jnp.where(strict, w, 0.0)
                acc_ref[h] = acc + jnp.broadcast_to(cs[:, 0:1], acc.shape)
                part = _dot(w.astype(BF16), vm[hh])
                o_pair = part if o_pair is None else o_pair + part
            o_ref[:, cols] += o_pair

    key_tile(qi, True)

    def body(it, carry):
        key_tile(qi - 1 - it, False)
        return carry

    lax.fori_loop(0, qi, body, 0)
    y_ref[0] = (o_ref[...] * _silu(z_ref[0].astype(F32))).astype(BF16)


def _sc_call(sc, tri):
    b, s, _ = sc.shape
    tile = SC_TILE
    nq = s // tile
    return pl.pallas_call(
        functools.partial(_sc_kernel, tile=tile),
        grid=(b, nq),
        in_specs=[
            pl.BlockSpec((1, tile, SC_WIDTH), lambda bi, qi: (bi, qi, 0)),
            pl.BlockSpec((1, s, SC_WIDTH), lambda bi, qi: (bi, 0, 1)),
            pl.BlockSpec((1, s, SC_WIDTH), lambda bi, qi: (bi, 0, 2)),
            pl.BlockSpec((1, tile, SC_WIDTH), lambda bi, qi: (bi, qi, 3)),
            pl.BlockSpec((2 * tile, tile), lambda bi, qi: (0, 0)),
        ],
        out_specs=pl.BlockSpec((1, tile, SC_WIDTH), lambda bi, qi: (bi, qi, 0)),
        out_shape=jax.ShapeDtypeStruct((b, s, SC_WIDTH), BF16),
        scratch_shapes=[
            pltpu.VMEM((SC_HEADS, tile, LANE), BF16),
            pltpu.VMEM((SC_HEADS, tile, LANE), F32),
            pltpu.VMEM((tile, SC_WIDTH), F32),
        ],
        compiler_params=_params(2),
        name="stick_breaking",
    )(sc, sc, sc, sc, tri)


def _out_kernel(x_ref, mod_ref, ya_ref, yb_ref, yc_ref, wa_ref, wb_ref, wc_ref, o_ref,
                *, d_model):
    y = _dot(ya_ref[0], wa_ref[...]) + _dot(yb_ref[0], wb_ref[...]) + _dot(yc_ref[0], wc_ref[...])
    gate = mod_ref[0, :, 2 * d_model:3 * d_model]
    o_ref[0] = x_ref[0] + gate * y


def _out_call(x, mod3, ya, yb, yc, wa, wb, wc):
    b, s, d = x.shape
    tm = 512 if s % 512 == 0 else s
    row = lambda bi, si: (bi, si, 0)
    const2 = lambda bi, si: (0, 0)
    return pl.pallas_call(
        functools.partial(_out_kernel, d_model=d),
        grid=(b, s // tm),
        in_specs=[
            pl.BlockSpec((1, tm, d), row),
            pl.BlockSpec((1, 1, 3 * d), lambda bi, si: (bi, 0, 0)),
            pl.BlockSpec((1, tm, GA_WIDTH), row),
            pl.BlockSpec((1, tm, MB_PWIDTH), row),
            pl.BlockSpec((1, tm, SC_WIDTH), row),
            pl.BlockSpec((GA_WIDTH, d), const2),
            pl.BlockSpec((MB_PWIDTH, d), const2),
            pl.BlockSpec((SC_WIDTH, d), const2),
        ],
        out_specs=pl.BlockSpec((1, tm, d), row),
        out_shape=jax.ShapeDtypeStruct((b, s, d), F32),
        compiler_params=_params(2),
        name="out_proj",
    )(x, mod3, ya, yb, yc, wa, wb, wc)


def _pad_heads(a):
    lead = a.shape[:-1]
    a = a.reshape(lead + (MB_HEADS, MB_HEAD))
    a = jnp.pad(a, [(0, 0)] * len(lead) + [(0, 0), (0, MB_PAD - MB_HEAD)])
    return a.reshape(lead + (MB_PWIDTH,))


def _block_mean_matrix(width, group):
    idx = jnp.arange(width) // group
    return jnp.where(idx[:, None] == idx[None, :], 1.0 / group, 0.0).astype(BF16)


def _layer_params(w_in, ga_v_norm, ga_bs, mb_conv_w, mb_conv_b, mb_b_i, mb_b_f, mb_h_norm,
                  sc_q_norm, sc_k_norm, w_out):
    d = w_in.shape[0]
    o_mb = N_GA
    o_gate = o_mb + 5 * MB_WIDTH
    o_sc = o_gate + 2 * MB_HEADS
    w_mb = jnp.concatenate(
        [_pad_heads(w_in[:, o_mb + i * MB_WIDTH:o_mb + (i + 1) * MB_WIDTH]) for i in range(5)], axis=1)
    w_gate = jnp.pad(w_in[:, o_gate:o_sc], ((0, 0), (0, GATE_W - 2 * MB_HEADS)))
    w_r = jnp.concatenate([w_in[:, :N_GA], w_mb, w_gate, w_in[:, o_sc:]], axis=1).astype(BF16)

    conv_w = jnp.concatenate([_pad_heads(mb_conv_w[:, :MB_WIDTH]), _pad_heads(mb_conv_w[:, MB_WIDTH:])], axis=1)
    conv_b = jnp.concatenate([_pad_heads(mb_conv_b[:MB_WIDTH]), _pad_heads(mb_conv_b[MB_WIDTH:])])[None, :]
    gate_b = jnp.pad(jnp.concatenate([mb_b_i, mb_b_f]), (0, GATE_W - 2 * MB_HEADS))[None, :]
    hnorm = _pad_heads(mb_h_norm.reshape(MB_WIDTH))[None, :]

    qkg = jnp.concatenate([jnp.tile(sc_q_norm, SC_HEADS) * (SC_HEAD ** -0.5 * LOG2E),
                           jnp.tile(sc_k_norm, SC_HEADS)])[None, :]
    vg = ga_v_norm.reshape(1, GA_WIDTH)
    bs_full = jnp.repeat(ga_bs.T, GA_HEAD, axis=1)

    wa = w_out[:GA_WIDTH].astype(BF16)
    wb = w_out[GA_WIDTH:GA_WIDTH + MB_WIDTH].reshape(MB_HEADS, MB_HEAD, d)
    wb = jnp.pad(wb, ((0, 0), (0, MB_PAD - MB_HEAD), (0, 0))).reshape(MB_PWIDTH, d).astype(BF16)
    wc = w_out[GA_WIDTH + MB_WIDTH:].astype(BF16)
    return dict(w_r=w_r, conv_w=conv_w, conv_b=conv_b, gate_b=gate_b, hnorm=hnorm, qkg=qkg,
                vg=vg, bs_full=bs_full, wa=wa, wb=wb, wc=wc)


def kernel(x, c, norm_g, w_ada, b_ada, w_in, ga_v_norm, ga_ws, ga_bs, mb_conv_w, mb_conv_b,
           mb_b_i, mb_b_f, mb_h_norm, sc_q_norm, sc_k_norm, w_out):
    depth = w_in.shape[0]
    b, s, d = x.shape
    assert s % SC_TILE == 0 and s % CHUNK == 0 and d % LANE == 0

    mod = _ada_call(c, w_ada, b_ada)
    gm_sc = _block_mean_matrix(2 * SC_WIDTH, SC_HEAD)
    gm_ga = _block_mean_matrix(GA_WIDTH, GA_HEAD)
    gm_ga = jnp.concatenate([gm_ga, gm_ga], axis=0)
    r = jnp.arange(CHUNK)
    tri_low = (r[None, :] <= r[:, None]).astype(F32)
    r2 = jnp.arange(SC_TILE)
    tri_suffix = (r2[:, None] >= r2[None, :]).astype(BF16)
    tri_suffix = jnp.concatenate([tri_suffix, tri_suffix], axis=0)

    for l in range(depth):
        p = _layer_params(w_in[l], ga_v_norm[l], ga_bs[l], mb_conv_w[l], mb_conv_b[l], mb_b_i[l],
                          mb_b_f[l], mb_h_norm[l], sc_q_norm[l], sc_k_norm[l], w_out[l])
        mod3 = mod[l][:, None, :]
        ga, mb, gates, sc = _inproj_call(x, mod3, norm_g[l][None, :], p["w_r"], p["qkg"], gm_sc)
        ya = _ga_call(ga, p["vg"], ga_ws[l], p["bs_full"], gm_ga)
        yb = _mb_call(mb, gates, p["conv_w"], p["conv_b"], p["gate_b"], p["hnorm"], tri_low)
        yc = _sc_call(sc, tri_suffix)
        x = _out_call(x, mod3, ya, yb, yc, p["wa"], p["wb"], p["wc"])
    return x
```

```python
import functools

import jax
import jax.numpy as jnp
from jax import lax
from jax.experimental import pallas as pl
from jax.experimental.pallas import tpu as pltpu

CHUNK = 128
GA_GROUPS = 4
GA_HEAD = 64
GA_WIDTH = GA_GROUPS * GA_HEAD
MB_HEADS = 4
MB_HEAD = 96
MB_PAD = 128
MB_WIDTH = MB_HEADS * MB_HEAD
MB_PWIDTH = MB_HEADS * MB_PAD
MB_CONV = 4
SC_HEADS = 6
SC_HEAD = 64
SC_WIDTH = SC_HEADS * SC_HEAD
GATE_W = 128
EPS = 1e-6
LANE = 128
SC_TILE = 256
VMEM_LIMIT = 56 * 1024 * 1024

F32 = jnp.float32
BF16 = jnp.bfloat16
HIGHEST = lax.Precision.HIGHEST


def _dot(a, b):
    return jnp.dot(a, b, preferred_element_type=F32)


def _dot_nt(a, b):
    return lax.dot_general(a, b, (((1,), (1,)), ((), ())), preferred_element_type=F32)


def _split_dot(a, b2):
    hi = a.astype(BF16)
    lo = (a - hi.astype(F32)).astype(BF16)
    return _dot(jnp.concatenate([hi, lo], axis=1), b2)


def _sigmoid(x):
    return 1.0 / (1.0 + jnp.exp(-x))


def _silu(x):
    return x * _sigmoid(x)


def _gelu_tanh(x):
    return 0.5 * x * (1.0 + jnp.tanh(0.7978845608028654 * (x + 0.044715 * (x * x * x))))


def _params(n_axes):
    return pltpu.CompilerParams(
        dimension_semantics=("arbitrary",) * n_axes, vmem_limit_bytes=VMEM_LIMIT)


def _ada_kernel(c_ref, w_ref, b_ref, o_ref):
    c = c_ref[...]
    o_ref[0] = jnp.dot(_silu(c), w_ref[0], precision=HIGHEST,
                       preferred_element_type=F32) + b_ref[0]


def _ada_call(c, w_ada, b_ada):
    depth, d, d3 = w_ada.shape
    b = c.shape[0]
    tn = 1024 if d3 % 1024 == 0 else d3
    return pl.pallas_call(
        _ada_kernel,
        grid=(depth, d3 // tn),
        in_specs=[
            pl.BlockSpec((b, d), lambda l, n: (0, 0)),
            pl.BlockSpec((1, d, tn), lambda l, n: (l, 0, n)),
            pl.BlockSpec((1, 1, tn), lambda l, n: (l, 0, n)),
        ],
        out_specs=pl.BlockSpec((1, b, tn), lambda l, n: (l, 0, n)),
        out_shape=jax.ShapeDtypeStruct((depth, b, d3), F32),
        compiler_params=_params(2),
        name="ada_mod",
    )(c, w_ada, b_ada.reshape(depth, 1, d3))


N_GA = 3 * GA_WIDTH
N_MB = 5 * MB_PWIDTH
N_SC = 4 * SC_WIDTH
OFF_MB = N_GA
OFF_GATE = OFF_MB + N_MB
OFF_SC = OFF_GATE + GATE_W
N_PROJ = OFF_SC + N_SC


def _inproj_kernel(x_ref, mod_ref, g_ref, w_ref, qkg_ref, gm_ref,
                   ga_ref, mb_ref, gate_ref, sc_ref, *, d_model):
    x = x_ref[0]
    ms = jnp.mean(x * x, axis=-1, keepdims=True)
    y = x * lax.rsqrt(ms + EPS) * g_ref[...]
    shift = mod_ref[0, :, 0:d_model]
    scale = mod_ref[0, :, d_model:2 * d_model]
    hb = (y * (1.0 + scale) + shift).astype(BF16)

    def proj(c0, n):
        return _dot(hb, w_ref[:, c0:c0 + n])

    for c0 in range(0, N_GA, 256):
        ga_ref[0, :, c0:c0 + 256] = proj(c0, 256).astype(BF16)
    for c0 in range(0, N_MB, 512):
        mb_ref[0, :, c0:c0 + 512] = proj(OFF_MB + c0, 512).astype(BF16)
    gate_ref[0] = proj(OFF_GATE, GATE_W)

    qk = proj(OFF_SC, 2 * SC_WIDTH)
    msq = _dot((qk * qk).astype(BF16), gm_ref[...])
    sc_ref[0, :, 0:2 * SC_WIDTH] = (qk * lax.rsqrt(msq + EPS) * qkg_ref[...]).astype(BF16)
    for c0 in range(2 * SC_WIDTH, N_SC, SC_WIDTH):
        sc_ref[0, :, c0:c0 + SC_WIDTH] = proj(OFF_SC + c0, SC_WIDTH).astype(BF16)


def _inproj_call(x, mod3, norm_g, w_r, qkg, gmat):
    b, s, d = x.shape
    tm = 512 if s % 512 == 0 else s
    row = lambda bi, si: (bi, si, 0)
    const2 = lambda bi, si: (0, 0)
    return pl.pallas_call(
        functools.partial(_inproj_kernel, d_model=d),
        grid=(b, s // tm),
        in_specs=[
            pl.BlockSpec((1, tm, d), row),
            pl.BlockSpec((1, 1, 3 * d), lambda bi, si: (bi, 0, 0)),
            pl.BlockSpec((1, d), const2),
            pl.BlockSpec((d, N_PROJ), const2),
            pl.BlockSpec((1, 2 * SC_WIDTH), const2),
            pl.BlockSpec((2 * SC_WIDTH, 2 * SC_WIDTH), const2),
        ],
        out_specs=[
            pl.BlockSpec((1, tm, N_GA), row),
            pl.BlockSpec((1, tm, N_MB), row),
            pl.BlockSpec((1, tm, GATE_W), row),
            pl.BlockSpec((1, tm, N_SC), row),
        ],
        out_shape=[
            jax.ShapeDtypeStruct((b, s, N_GA), BF16),
            jax.ShapeDtypeStruct((b, s, N_MB), BF16),
            jax.ShapeDtypeStruct((b, s, GATE_W), F32),
            jax.ShapeDtypeStruct((b, s, N_SC), BF16),
        ],
        compiler_params=_params(2),
        name="in_proj",
    )(x, mod3, norm_g, w_r, qkg, gmat)


GA_CHUNKS = 4


def _ga_kernel(u_ref, v_ref, z_ref, vg_ref, ws_ref, bs_ref, gm_ref, y_ref, *, nchunk):
    t_idx = lax.broadcasted_iota(jnp.int32, (CHUNK, CHUNK), 0)
    s_idx = lax.broadcasted_iota(jnp.int32, (CHUNK, CHUNK), 1)
    lane = lax.broadcasted_iota(jnp.int32, (CHUNK, GA_WIDTH), 1)
    w_causal = [jnp.where(s_idx <= t_idx, ws_ref[g], 0.0).astype(BF16) for g in range(GA_GROUPS)]
    for c in range(nchunk):
        rows = slice(c * CHUNK, (c + 1) * CHUNK)
        u = _gelu_tanh(u_ref[0, rows, :].astype(F32))
        v = _gelu_tanh(v_ref[0, rows, :].astype(F32))
        msq = _split_dot(v * v, gm_ref[...])
        vn = (v * lax.rsqrt(msq + EPS) * vg_ref[...]).astype(BF16)
        sp = bs_ref[...]
        for g in range(GA_GROUPS):
            in_group = (lane >= g * GA_HEAD) & (lane < (g + 1) * GA_HEAD)
            sp = sp + _dot(w_causal[g], jnp.where(in_group, vn, jnp.zeros_like(vn)))
        y_ref[0, rows, :] = (u * sp * _silu(z_ref[0, rows, :].astype(F32))).astype(BF16)


def _ga_call(ga, vg, ws, bs_full, gmat):
    b, s, _ = ga.shape
    nchunk = GA_CHUNKS if s % (GA_CHUNKS * CHUNK) == 0 else 1
    rows = nchunk * CHUNK
    col = lambda j: (lambda bi, ci: (bi, ci, j))
    const2 = lambda bi, ci: (0, 0)
    return pl.pallas_call(
        functools.partial(_ga_kernel, nchunk=nchunk),
        grid=(b, s // rows),
        in_specs=[
            pl.BlockSpec((1, rows, GA_WIDTH), col(0)),
            pl.BlockSpec((1, rows, GA_WIDTH), col(1)),
            pl.BlockSpec((1, rows, GA_WIDTH), col(2)),
            pl.BlockSpec((1, GA_WIDTH), const2),
            pl.BlockSpec((GA_GROUPS, CHUNK, CHUNK), lambda bi, ci: (0, 0, 0)),
            pl.BlockSpec((CHUNK, GA_WIDTH), const2),
            pl.BlockSpec((2 * GA_WIDTH, GA_WIDTH), const2),
        ],
        out_specs=pl.BlockSpec((1, rows, GA_WIDTH), col(0)),
        out_shape=jax.ShapeDtypeStruct((b, s, GA_WIDTH), BF16),
        compiler_params=_params(2),
        name="gmlp",
    )(ga, ga, ga, vg, ws, bs_full, gmat)


MB_NB = 2


def _mb_kernel(q_ref, k_ref, v_ref, o_ref, z_ref, gate_ref, cw_ref, cb_ref, gb_ref,
               hn_ref, tri_ref, y_ref, c_state, m_state, ext_ref, *, nb):
    ci = pl.program_id(1)

    @pl.when(ci == 0)
    def _():
        c_state[...] = jnp.zeros_like(c_state)
        m_state[...] = jnp.zeros_like(m_state)
        ext_ref[:, 0:8, :] = jnp.zeros((nb, 8, 2 * MB_PWIDTH), F32)

    row = lax.broadcasted_iota(jnp.int32, (CHUNK, CHUNK), 0)
    col = lax.broadcasted_iota(jnp.int32, (CHUNK, CHUNK), 1)
    causal_t = row <= col

    for bi in range(nb):
        qk_raw = jnp.concatenate([q_ref[bi], k_ref[bi]], axis=1).astype(F32)
        ext_ref[bi, 8:8 + CHUNK, :] = qk_raw
        conv = cb_ref[...] + cw_ref[MB_CONV - 1:MB_CONV, :] * qk_raw
        for back in range(1, MB_CONV):
            tap = MB_CONV - 1 - back
            conv = conv + cw_ref[tap:tap + 1, :] * ext_ref[bi, 8 - back:8 - back + CHUNK, :]
        ext_ref[bi, 0:8, :] = qk_raw[CHUNK - 8:CHUNK, :]
        qk = _silu(conv)
        q_all = qk[:, 0:MB_PWIDTH].astype(BF16)
        k_all = (qk[:, MB_PWIDTH:2 * MB_PWIDTH] * (MB_HEAD ** -0.5)).astype(BF16)

        graw = gate_ref[bi] + gb_ref[...]
        lane_g = lax.broadcasted_iota(jnp.int32, graw.shape, 1)
        log_f = jnp.minimum(graw, 0.0) - jnp.log1p(jnp.exp(-jnp.abs(graw)))
        gm = jnp.where((lane_g >= MB_HEADS) & (lane_g < 2 * MB_HEADS), log_f, graw)
        bc = jnp.dot(tri_ref[...], gm, precision=HIGHEST, preferred_element_type=F32)
        gm_t = gm.T
        bc_t = bc.T

        for h in range(MB_HEADS):
            cs = slice(h * MB_PAD, (h + 1) * MB_PAD)
            b_row = bc_t[MB_HEADS + h:MB_HEADS + h + 1, :]
            a_row = gm_t[h:h + 1, :] - b_row
            a_col = gm[:, h:h + 1] - bc[:, MB_HEADS + h:MB_HEADS + h + 1]
            m_prev = m_state[bi, h:h + 1, :]

            d_log = jnp.where(causal_t, b_row + a_col, -jnp.inf)
            inter = b_row + m_prev
            m_t = jnp.maximum(inter, jnp.max(d_log, axis=0, keepdims=True))
            w_intra = jnp.exp(d_log - m_t)
            w_inter = jnp.exp(inter - m_t)

            qh = q_all[:, cs]
            kh = k_all[:, cs]
            v_t = v_ref[bi, :, cs].astype(F32).T
            v_aug = jnp.where(row == MB_HEAD, 1.0, v_t)
            s_mat = _dot_nt(kh, qh) * w_intra
            num = _dot(v_aug.astype(BF16), s_mat.astype(BF16)) \
                + w_inter * _dot_nt(c_state[bi, h].astype(BF16), qh)
            den = num[MB_HEAD:MB_HEAD + 1, :]
            hh = num * (1.0 / jnp.maximum(jnp.abs(den), jnp.exp(-m_t)))
            hh = jnp.where(row < MB_HEAD, hh, 0.0)

            b_tot = b_row[:, CHUNK - 1:CHUNK]
            m_new = m_t[:, CHUNK - 1:CHUNK]
            decay = jnp.exp(b_tot + m_prev[:, 0:1] - m_new)
            ws_row = jnp.exp(b_tot + a_row - m_new)
            c_state[bi, h] = decay * c_state[bi, h] + _dot((v_aug * ws_row).astype(BF16), kh)
            m_state[bi, h:h + 1, :] = jnp.broadcast_to(m_new, (1, LANE))

            msq = jnp.sum(hh * hh, axis=0, keepdims=True) * (1.0 / MB_HEAD)
            hn = (hh * lax.rsqrt(msq + EPS) * hn_ref[h]).T
            y = _sigmoid(o_ref[bi, :, cs].astype(F32)) * hn * _silu(z_ref[bi, :, cs].astype(F32))
            y_ref[bi, :, cs] = y.astype(BF16)


def _mb_call(mb, gates, conv_w, conv_b, gate_b, hnorm, tri):
    b, s, _ = mb.shape
    nc = s // CHUNK
    nb = MB_NB if b % MB_NB == 0 else 1
    col = lambda j: (lambda bi, ci: (bi, ci, j))
    const2 = lambda bi, ci: (0, 0)
    return pl.pallas_call(
        functools.partial(_mb_kernel, nb=nb),
        grid=(b // nb, nc),
        in_specs=[
            pl.BlockSpec((nb, CHUNK, MB_PWIDTH), col(0)),
            pl.BlockSpec((nb, CHUNK, MB_PWIDTH), col(1)),
            pl.BlockSpec((nb, CHUNK, MB_PWIDTH), col(2)),
            pl.BlockSpec((nb, CHUNK, MB_PWIDTH), col(3)),
            pl.BlockSpec((nb, CHUNK, MB_PWIDTH), col(4)),
            pl.BlockSpec((nb, CHUNK, GATE_W), col(0)),
            pl.BlockSpec((MB_CONV, 2 * MB_PWIDTH), const2),
            pl.BlockSpec((1, 2 * MB_PWIDTH), const2),
            pl.BlockSpec((1, GATE_W), const2),
            pl.BlockSpec((MB_HEADS, MB_PAD, CHUNK), lambda bi, ci: (0, 0, 0)),
            pl.BlockSpec((CHUNK, CHUNK), const2),
        ],
        out_specs=pl.BlockSpec((nb, CHUNK, MB_PWIDTH), col(0)),
        out_shape=jax.ShapeDtypeStruct((b, s, MB_PWIDTH), BF16),
        scratch_shapes=[
            pltpu.VMEM((nb, MB_HEADS, MB_PAD, MB_PAD), F32),
            pltpu.VMEM((nb, 8, LANE), F32),
            pltpu.VMEM((nb, 8 + CHUNK, 2 * MB_PWIDTH), F32),
        ],
        compiler_params=_params(2),
        name="mlstm",
    )(mb, mb, mb, mb, mb, gates, conv_w, conv_b, gate_b, hnorm, tri)


LOG2E = 1.4426950408889634
SC_QTILE = 512


def _sc_kernel(q_ref, k_ref, v_ref, z_ref, tri_ref, y_ref, qm_ref, acc_ref, o_ref, zs_ref, cs_ref,
               sb_ref, *, tile, nsub):
    qi = pl.program_id(1)
    npair = SC_WIDTH // LANE
    low = lax.broadcasted_iota(jnp.int32, (tile, LANE), 1) < SC_HEAD
    for u in range(nsub):
        for p in range(npair):
            q = q_ref[0, u * tile:(u + 1) * tile, p * LANE:(p + 1) * LANE]
            zero = jnp.zeros_like(q)
            qm_ref[u * SC_HEADS + 2 * p] = jnp.where(low, q, zero)
            qm_ref[u * SC_HEADS + 2 * p + 1] = jnp.where(low, zero, q)
    acc_ref[...] = jnp.zeros_like(acc_ref)
    o_ref[...] = jnp.zeros_like(o_ref)

    def key_tile(j, modes):
        start = pl.multiple_of(j * tile, tile)
        if 'diag' in modes:
            t_idx = lax.broadcasted_iota(jnp.int32, (tile, tile), 0)
            s_idx = lax.broadcasted_iota(jnp.int32, (tile, tile), 1)
            strict = s_idx < t_idx
        for p in range(npair):
            cols = slice(p * LANE, (p + 1) * LANE)
            kt = k_ref[0, pl.ds(start, tile), cols]
            vt = v_ref[0, pl.ds(start, tile), cols]
            zero_v = jnp.zeros_like(vt)
            vm = (jnp.where(low, vt, zero_v), jnp.where(low, zero_v, vt))
            chains = [(u, hh) for u in range(nsub) if modes[u] is not None for hh in range(2)]
            idx = [u * SC_HEADS + 2 * p + hh for (u, hh) in chains]
            for c in idx:
                zs_ref[c] = _dot_nt(qm_ref[c], kt)
            for c, (u, hh) in zip(idx, chains):
                z = zs_ref[c]
                sp = jnp.maximum(z, 0.0) + jnp.log(1.0 + jnp.exp2(-jnp.abs(z))) * LOG2E
                if modes[u] == 'diag':
                    sp = jnp.where(strict, sp, 0.0)
                sb_ref[c] = sp.astype(BF16)
            for c in idx:
                cs_ref[c] = _dot(sb_ref[c], tri_ref[...])
            for c, (u, hh) in zip(idx, chains):
                cs = cs_ref[c]
                acc = acc_ref[c]
                w = jnp.exp2(zs_ref[c] - cs - jnp.concatenate([acc] * (tile // LANE), axis=1))
                if modes[u] == 'diag':
                    w = jnp.where(strict, w, 0.0)
                acc_ref[c] = acc + jnp.broadcast_to(cs[:, 0:1], acc.shape)
                sb_ref[c] = w.astype(BF16)
            for c, (u, hh) in zip(idx, chains):
                o_ref[u * tile:(u + 1) * tile, cols] += _dot(sb_ref[c], vm[hh])

    for c in reversed(range(nsub)):
        key_tile(qi * nsub + c, [None if u < c else ('diag' if u == c else 'full') for u in range(nsub)])

    def body(it, carry):
        key_tile(qi * nsub - 1 - it, ['full'] * nsub)
        return carry

    lax.fori_loop(0, qi * nsub, body, 0)
    y_ref[0] = (o_ref[...] * _silu(z_ref[0].astype(F32))).astype(BF16)


def _sc_call(sc, tri):
    b, s, _ = sc.shape
    tile = SC_TILE
    tq = SC_QTILE if s % SC_QTILE == 0 else tile
    nsub = tq // tile
    return pl.pallas_call(
        functools.partial(_sc_kernel, tile=tile, nsub=nsub),
        grid=(b, s // tq),
        in_specs=[
            pl.BlockSpec((1, tq, SC_WIDTH), lambda bi, qi: (bi, qi, 0)),
            pl.BlockSpec((1, s, SC_WIDTH), lambda bi, qi: (bi, 0, 1)),
            pl.BlockSpec((1, s, SC_WIDTH), lambda bi, qi: (bi, 0, 2)),
            pl.BlockSpec((1, tq, SC_WIDTH), lambda bi, qi: (bi, qi, 3)),
            pl.BlockSpec((tile, tile), lambda bi, qi: (0, 0)),
        ],
        out_specs=pl.BlockSpec((1, tq, SC_WIDTH), lambda bi, qi: (bi, qi, 0)),
        out_shape=jax.ShapeDtypeStruct((b, s, SC_WIDTH), BF16),
        scratch_shapes=[
            pltpu.VMEM((nsub * SC_HEADS, tile, LANE), BF16),
            pltpu.VMEM((nsub * SC_HEADS, tile, LANE), F32),
            pltpu.VMEM((tq, SC_WIDTH), F32),
            pltpu.VMEM((nsub * SC_HEADS, tile, tile), F32),
            pltpu.VMEM((nsub * SC_HEADS, tile, tile), F32),
            pltpu.VMEM((nsub * SC_HEADS, tile, tile), BF16),
        ],
        compiler_params=_params(2),
        name="stick_breaking",
    )(sc, sc, sc, sc, tri)


def _out_kernel(x_ref, mod_ref, ya_ref, yb_ref, yc_ref, wa_ref, wb_ref, wc_ref, o_ref,
                *, d_model):
    y = _dot(ya_ref[0], wa_ref[...]) + _dot(yb_ref[0], wb_ref[...]) + _dot(yc_ref[0], wc_ref[...])
    gate = mod_ref[0, :, 2 * d_model:3 * d_model]
    o_ref[0] = x_ref[0] + gate * y


def _out_call(x, mod3, ya, yb, yc, wa, wb, wc):
    b, s, d = x.shape
    tm = 512 if s % 512 == 0 else s
    row = lambda bi, si: (bi, si, 0)
    const2 = lambda bi, si: (0, 0)
    return pl.pallas_call(
        functools.partial(_out_kernel, d_model=d),
        grid=(b, s // tm),
        in_specs=[
            pl.BlockSpec((1, tm, d), row),
            pl.BlockSpec((1, 1, 3 * d), lambda bi, si: (bi, 0, 0)),
            pl.BlockSpec((1, tm, GA_WIDTH), row),
            pl.BlockSpec((1, tm, MB_PWIDTH), row),
            pl.BlockSpec((1, tm, SC_WIDTH), row),
            pl.BlockSpec((GA_WIDTH, d), const2),
            pl.BlockSpec((MB_PWIDTH, d), const2),
            pl.BlockSpec((SC_WIDTH, d), const2),
        ],
        out_specs=pl.BlockSpec((1, tm, d), row),
        out_shape=jax.ShapeDtypeStruct((b, s, d), F32),
        compiler_params=_params(2),
        name="out_proj",
    )(x, mod3, ya, yb, yc, wa, wb, wc)


def _pad_heads(a):
    lead = a.shape[:-1]
    a = a.reshape(lead + (MB_HEADS, MB_HEAD))
    a = jnp.pad(a, [(0, 0)] * len(lead) + [(0, 0), (0, MB_PAD - MB_HEAD)])
    return a.reshape(lead + (MB_PWIDTH,))


def _block_mean_matrix(width, group):
    idx = jnp.arange(width) // group
    return jnp.where(idx[:, None] == idx[None, :], 1.0 / group, 0.0).astype(BF16)


def _layer_params(w_in, ga_v_norm, ga_bs, mb_conv_w, mb_conv_b, mb_b_i, mb_b_f, mb_h_norm,
                  sc_q_norm, sc_k_norm, w_out):
    d = w_in.shape[0]
    o_mb = N_GA
    o_gate = o_mb + 5 * MB_WIDTH
    o_sc = o_gate + 2 * MB_HEADS
    w_mb = jnp.concatenate(
        [_pad_heads(w_in[:, o_mb + i * MB_WIDTH:o_mb + (i + 1) * MB_WIDTH]) for i in range(5)], axis=1)
    w_gate = jnp.pad(w_in[:, o_gate:o_sc], ((0, 0), (0, GATE_W - 2 * MB_HEADS)))
    w_r = jnp.concatenate([w_in[:, :N_GA], w_mb, w_gate, w_in[:, o_sc:]], axis=1).astype(BF16)

    conv_w = jnp.concatenate([_pad_heads(mb_conv_w[:, :MB_WIDTH]), _pad_heads(mb_conv_w[:, MB_WIDTH:])], axis=1)
    conv_b = jnp.concatenate([_pad_heads(mb_conv_b[:MB_WIDTH]), _pad_heads(mb_conv_b[MB_WIDTH:])])[None, :]
    gate_b = jnp.pad(jnp.concatenate([mb_b_i, mb_b_f]), (0, GATE_W - 2 * MB_HEADS))[None, :]
    hnorm = jnp.pad(mb_h_norm, ((0, 0), (0, MB_PAD - MB_HEAD)))
    hnorm = jnp.broadcast_to(hnorm[:, :, None], (MB_HEADS, MB_PAD, CHUNK))

    qkg = jnp.concatenate([jnp.tile(sc_q_norm, SC_HEADS) * (SC_HEAD ** -0.5 * LOG2E),
                           jnp.tile(sc_k_norm, SC_HEADS)])[None, :]
    vg = ga_v_norm.reshape(1, GA_WIDTH)
    bs_full = jnp.repeat(ga_bs.T, GA_HEAD, axis=1)

    wa = w_out[:GA_WIDTH].astype(BF16)
    wb = w_out[GA_WIDTH:GA_WIDTH + MB_WIDTH].reshape(MB_HEADS, MB_HEAD, d)
    wb = jnp.pad(wb, ((0, 0), (0, MB_PAD - MB_HEAD), (0, 0))).reshape(MB_PWIDTH, d).astype(BF16)
    wc = w_out[GA_WIDTH + MB_WIDTH:].astype(BF16)
    return dict(w_r=w_r, conv_w=conv_w, conv_b=conv_b, gate_b=gate_b, hnorm=hnorm, qkg=qkg,
                vg=vg, bs_full=bs_full, wa=wa, wb=wb, wc=wc)


def kernel(x, c, norm_g, w_ada, b_ada, w_in, ga_v_norm, ga_ws, ga_bs, mb_conv_w, mb_conv_b,
           mb_b_i, mb_b_f, mb_h_norm, sc_q_norm, sc_k_norm, w_out):
    depth = w_in.shape[0]
    b, s, d = x.shape
    assert s % SC_TILE == 0 and s % CHUNK == 0 and d % LANE == 0

    mod = _ada_call(c, w_ada, b_ada)
    gm_sc = _block_mean_matrix(2 * SC_WIDTH, SC_HEAD)
    gm_ga = _block_mean_matrix(GA_WIDTH, GA_HEAD)
    gm_ga = jnp.concatenate([gm_ga, gm_ga], axis=0)
    r = jnp.arange(CHUNK)
    tri_low = (r[None, :] <= r[:, None]).astype(F32)
    r2 = jnp.arange(SC_TILE)
    tri_suffix = (r2[:, None] >= r2[None, :]).astype(BF16)

    for l in range(depth):
        p = _layer_params(w_in[l], ga_v_norm[l], ga_bs[l], mb_conv_w[l], mb_conv_b[l], mb_b_i[l],
                          mb_b_f[l], mb_h_norm[l], sc_q_norm[l], sc_k_norm[l], w_out[l])
        mod3 = mod[l][:, None, :]
        ga, mb, gates, sc = _inproj_call(x, mod3, norm_g[l][None, :], p["w_r"], p["qkg"], gm_sc)
        ya = _ga_call(ga, p["vg"], ga_ws[l], p["bs_full"], gm_ga)
        yb = _mb_call(mb, gates, p["conv_w"], p["conv_b"], p["gate_b"], p["hnorm"], tri_low)
        yc = _sc_call(sc, tri_suffix)
        x = _out_call(x, mod3, ya, yb, yc, p["wa"], p["wb"], p["wc"])
    return x
```

```python
import functools

import jax
import jax.numpy as jnp
from jax import lax
from jax.experimental import pallas as pl
from jax.experimental.pallas import tpu as pltpu

CHUNK = 128
GA_GROUPS = 4
GA_HEAD = 64
GA_WIDTH = GA_GROUPS * GA_HEAD
MB_HEADS = 4
MB_HEAD = 96
MB_PAD = 128
MB_WIDTH = MB_HEADS * MB_HEAD
MB_PWIDTH = MB_HEADS * MB_PAD
MB_CONV = 4
SC_HEADS = 6
SC_HEAD = 64
SC_WIDTH = SC_HEADS * SC_HEAD
GATE_W = 128
EPS = 1e-6
LANE = 128
SC_TILE = 256
VMEM_LIMIT = 56 * 1024 * 1024

F32 = jnp.float32
BF16 = jnp.bfloat16
HIGHEST = lax.Precision.HIGHEST


def _dot(a, b):
    return jnp.dot(a, b, preferred_element_type=F32)


def _dot_nt(a, b):
    return lax.dot_general(a, b, (((1,), (1,)), ((), ())), preferred_element_type=F32)


def _split_dot(a, b2):
    hi = a.astype(BF16)
    lo = (a - hi.astype(F32)).astype(BF16)
    return _dot(jnp.concatenate([hi, lo], axis=1), b2)


def _sigmoid(x):
    return 1.0 / (1.0 + jnp.exp(-x))


def _silu(x):
    return x * _sigmoid(x)


def _gelu_tanh(x):
    return 0.5 * x * (1.0 + jnp.tanh(0.7978845608028654 * (x + 0.044715 * (x * x * x))))


def _params(n_axes):
    return pltpu.CompilerParams(
        dimension_semantics=("arbitrary",) * n_axes, vmem_limit_bytes=VMEM_LIMIT)


def _ada_kernel(c_ref, w_ref, b_ref, o_ref):
    c = c_ref[...]
    o_ref[0] = jnp.dot(_silu(c), w_ref[0], precision=HIGHEST,
                       preferred_element_type=F32) + b_ref[0]


def _ada_call(c, w_ada, b_ada):
    depth, d, d3 = w_ada.shape
    b = c.shape[0]
    tn = 1024 if d3 % 1024 == 0 else d3
    return pl.pallas_call(
        _ada_kernel,
        grid=(depth, d3 // tn),
        in_specs=[
            pl.BlockSpec((b, d), lambda l, n: (0, 0)),
            pl.BlockSpec((1, d, tn), lambda l, n: (l, 0, n)),
            pl.BlockSpec((1, 1, tn), lambda l, n: (l, 0, n)),
        ],
        out_specs=pl.BlockSpec((1, b, tn), lambda l, n: (l, 0, n)),
        out_shape=jax.ShapeDtypeStruct((depth, b, d3), F32),
        compiler_params=_params(2),
        name="ada_mod",
    )(c, w_ada, b_ada.reshape(depth, 1, d3))


N_GA = 3 * GA_WIDTH
N_MB = 5 * MB_PWIDTH
N_SC = 4 * SC_WIDTH
OFF_MB = N_GA
OFF_GATE = OFF_MB + N_MB
OFF_SC = OFF_GATE + GATE_W
N_PROJ = OFF_SC + N_SC


def _inproj_kernel(x_ref, mod_ref, g_ref, w_ref, qkg_ref, gm_ref, vg_ref, ws_ref, bs_ref, gmga_ref,
                   ya_ref, mb_ref, gate_ref, sc_ref, *, d_model, tm):
    x = x_ref[0]
    ms = jnp.mean(x * x, axis=-1, keepdims=True)
    y = x * lax.rsqrt(ms + EPS) * g_ref[...]
    shift = mod_ref[0, :, 0:d_model]
    scale = mod_ref[0, :, d_model:2 * d_model]
    hb = (y * (1.0 + scale) + shift).astype(BF16)

    def proj(c0, n):
        return _dot(hb, w_ref[:, c0:c0 + n])

    for c0 in range(0, N_MB, MB_PWIDTH):
        mb_ref[0, :, c0:c0 + MB_PWIDTH] = proj(OFF_MB + c0, MB_PWIDTH).astype(BF16)
    gate_ref[0] = proj(OFF_GATE, GATE_W)

    qk = proj(OFF_SC, 2 * SC_WIDTH)
    msq = _dot((qk * qk).astype(BF16), gm_ref[...])
    sc_ref[0, :, 0:2 * SC_WIDTH] = (qk * lax.rsqrt(msq + EPS) * qkg_ref[...]).astype(BF16)
    for c0 in range(2 * SC_WIDTH, N_SC, SC_WIDTH):
        sc_ref[0, :, c0:c0 + SC_WIDTH] = proj(OFF_SC + c0, SC_WIDTH).astype(BF16)

    u_all = _gelu_tanh(proj(0, GA_WIDTH))
    v_all = _gelu_tanh(proj(GA_WIDTH, GA_WIDTH))
    z_all = proj(2 * GA_WIDTH, GA_WIDTH)
    t_idx = lax.broadcasted_iota(jnp.int32, (CHUNK, CHUNK), 0)
    s_idx = lax.broadcasted_iota(jnp.int32, (CHUNK, CHUNK), 1)
    lane = lax.broadcasted_iota(jnp.int32, (CHUNK, GA_WIDTH), 1)
    w_causal = [jnp.where(s_idx <= t_idx, ws_ref[g], 0.0).astype(BF16) for g in range(GA_GROUPS)]
    for c in range(tm // CHUNK):
        rows = slice(c * CHUNK, (c + 1) * CHUNK)
        v = v_all[rows]
        msq_v = _split_dot(v * v, gmga_ref[...])
        vn = (v * lax.rsqrt(msq_v + EPS) * vg_ref[...]).astype(BF16)
        sp = bs_ref[...]
        for g in range(GA_GROUPS):
            in_group = (lane >= g * GA_HEAD) & (lane < (g + 1) * GA_HEAD)
            sp = sp + _dot(w_causal[g], jnp.where(in_group, vn, jnp.zeros_like(vn)))
        ya_ref[0, rows, :] = (u_all[rows] * sp * _silu(z_all[rows])).astype(BF16)


def _inproj_call(x, mod3, norm_g, w_r, qkg, gmat, vg, ws, bs_full, gm_ga):
    b, s, d = x.shape
    tm = 512 if s % 512 == 0 else s
    assert tm % CHUNK == 0
    row = lambda bi, si: (bi, si, 0)
    const2 = lambda bi, si: (0, 0)
    return pl.pallas_call(
        functools.partial(_inproj_kernel, d_model=d, tm=tm),
        grid=(b, s // tm),
        in_specs=[
            pl.BlockSpec((1, tm, d), row),
            pl.BlockSpec((1, 1, 3 * d), lambda bi, si: (bi, 0, 0)),
            pl.BlockSpec((1, d), const2),
            pl.BlockSpec((d, N_PROJ), const2),
            pl.BlockSpec((1, 2 * SC_WIDTH), const2),
            pl.BlockSpec((2 * SC_WIDTH, 2 * SC_WIDTH), const2),
            pl.BlockSpec((1, GA_WIDTH), const2),
            pl.BlockSpec((GA_GROUPS, CHUNK, CHUNK), lambda bi, si: (0, 0, 0)),
            pl.BlockSpec((CHUNK, GA_WIDTH), const2),
            pl.BlockSpec((2 * GA_WIDTH, GA_WIDTH), const2),
        ],
        out_specs=[
            pl.BlockSpec((1, tm, GA_WIDTH), row),
            pl.BlockSpec((1, tm, N_MB), row),
            pl.BlockSpec((1, tm, GATE_W), row),
            pl.BlockSpec((1, tm, N_SC), row),
        ],
        out_shape=[
            jax.ShapeDtypeStruct((b, s, GA_WIDTH), BF16),
            jax.ShapeDtypeStruct((b, s, N_MB), BF16),
            jax.ShapeDtypeStruct((b, s, GATE_W), F32),
            jax.ShapeDtypeStruct((b, s, N_SC), BF16),
        ],
        compiler_params=_params(2),
        name="in_proj",
    )(x, mod3, norm_g, w_r, qkg, gmat, vg, ws, bs_full, gm_ga)


MB_NB = 4


def _mb_kernel(q_ref, k_ref, v_ref, o_ref, z_ref, gate_ref, cw_ref, cb_ref, gb_ref,
               hn_ref, tri_ref, y_ref, c_state, m_state, ext_ref, *, nb):
    ci = pl.program_id(1)

    @pl.when(ci == 0)
    def _():
        c_state[...] = jnp.zeros_like(c_state)
        m_state[...] = jnp.zeros_like(m_state)
        ext_ref[:, 0:8, :] = jnp.zeros((nb, 8, 2 * MB_PWIDTH), F32)

    row = lax.broadcasted_iota(jnp.int32, (CHUNK, CHUNK), 0)
    col = lax.broadcasted_iota(jnp.int32, (CHUNK, CHUNK), 1)
    causal_t = row <= col

    for bi in range(nb):
        qk_raw = jnp.concatenate([q_ref[bi], k_ref[bi]], axis=1).astype(F32)
        ext_ref[bi, 8:8 + CHUNK, :] = qk_raw
        conv = cb_ref[...] + cw_ref[MB_CONV - 1:MB_CONV, :] * qk_raw
        for back in range(1, MB_CONV):
            tap = MB_CONV - 1 - back
            conv = conv + cw_ref[tap:tap + 1, :] * ext_ref[bi, 8 - back:8 - back + CHUNK, :]
        ext_ref[bi, 0:8, :] = qk_raw[CHUNK - 8:CHUNK, :]
        qk = _silu(conv)
        q_all = qk[:, 0:MB_PWIDTH].astype(BF16)
        k_all = (qk[:, MB_PWIDTH:2 * MB_PWIDTH] * (MB_HEAD ** -0.5)).astype(BF16)

        graw = gate_ref[bi] + gb_ref[...]
        lane_g = lax.broadcasted_iota(jnp.int32, graw.shape, 1)
        log_f = jnp.minimum(graw, 0.0) - jnp.log1p(jnp.exp(-jnp.abs(graw)))
        gm = jnp.where((lane_g >= MB_HEADS) & (lane_g < 2 * MB_HEADS), log_f, graw)
        bc = jnp.dot(tri_ref[...], gm, precision=HIGHEST, preferred_element_type=F32)
        gm_t = gm.T
        bc_t = bc.T

        for h in range(MB_HEADS):
            cs = slice(h * MB_PAD, (h + 1) * MB_PAD)
            b_row = bc_t[MB_HEADS + h:MB_HEADS + h + 1, :]
            a_row = gm_t[h:h + 1, :] - b_row
            a_col = gm[:, h:h + 1] - bc[:, MB_HEADS + h:MB_HEADS + h + 1]
            m_prev = m_state[bi, h:h + 1, :]

            d_log = jnp.where(causal_t, b_row + a_col, -jnp.inf)
            inter = b_row + m_prev
            m_t = jnp.maximum(inter, jnp.max(d_log, axis=0, keepdims=True))
            w_intra = jnp.exp(d_log - m_t)
            w_inter = jnp.exp(inter - m_t)

            qh = q_all[:, cs]
            kh = k_all[:, cs]
            v_t = v_ref[bi, :, cs].astype(F32).T
            v_aug = jnp.where(row == MB_HEAD, 1.0, v_t)
            s_mat = _dot_nt(kh, qh) * w_intra
            num = _dot(v_aug.astype(BF16), s_mat.astype(BF16)) \
                + w_inter * _dot_nt(c_state[bi, h].astype(BF16), qh)
            den = num[MB_HEAD:MB_HEAD + 1, :]
            hh = num * (1.0 / jnp.maximum(jnp.abs(den), jnp.exp(-m_t)))
            hh = jnp.where(row < MB_HEAD, hh, 0.0)

            b_tot = b_row[:, CHUNK - 1:CHUNK]
            m_new = m_t[:, CHUNK - 1:CHUNK]
            decay = jnp.exp(b_tot + m_prev[:, 0:1] - m_new)
            ws_row = jnp.exp(b_tot + a_row - m_new)
            c_state[bi, h] = decay * c_state[bi, h] + _dot((v_aug * ws_row).astype(BF16), kh)
            m_state[bi, h:h + 1, :] = jnp.broadcast_to(m_new, (1, LANE))

            msq = jnp.sum(hh * hh, axis=0, keepdims=True) * (1.0 / MB_HEAD)
            hn = (hh * lax.rsqrt(msq + EPS) * hn_ref[h]).T
            y = _sigmoid(o_ref[bi, :, cs].astype(F32)) * hn * _silu(z_ref[bi, :, cs].astype(F32))
            y_ref[bi, :, cs] = y.astype(BF16)


def _mb_call(mb, gates, conv_w, conv_b, gate_b, hnorm, tri):
    b, s, _ = mb.shape
    nc = s // CHUNK
    nb = MB_NB if b % MB_NB == 0 else 1
    col = lambda j: (lambda bi, ci: (bi, ci, j))
    const2 = lambda bi, ci: (0, 0)
    return pl.pallas_call(
        functools.partial(_mb_kernel, nb=nb),
        grid=(b // nb, nc),
        in_specs=[
            pl.BlockSpec((nb, CHUNK, MB_PWIDTH), col(0)),
            pl.BlockSpec((nb, CHUNK, MB_PWIDTH), col(1)),
            pl.BlockSpec((nb, CHUNK, MB_PWIDTH), col(2)),
            pl.BlockSpec((nb, CHUNK, MB_PWIDTH), col(3)),
            pl.BlockSpec((nb, CHUNK, MB_PWIDTH), col(4)),
            pl.BlockSpec((nb, CHUNK, GATE_W), col(0)),
            pl.BlockSpec((MB_CONV, 2 * MB_PWIDTH), const2),
            pl.BlockSpec((1, 2 * MB_PWIDTH), const2),
            pl.BlockSpec((1, GATE_W), const2),
            pl.BlockSpec((MB_HEADS, MB_PAD, CHUNK), lambda bi, ci: (0, 0, 0)),
            pl.BlockSpec((CHUNK, CHUNK), const2),
        ],
        out_specs=pl.BlockSpec((nb, CHUNK, MB_PWIDTH), col(0)),
        out_shape=jax.ShapeDtypeStruct((b, s, MB_PWIDTH), BF16),
        scratch_shapes=[
            pltpu.VMEM((nb, MB_HEADS, MB_PAD, MB_PAD), F32),
            pltpu.VMEM((nb, 8, LANE), F32),
            pltpu.VMEM((nb, 8 + CHUNK, 2 * MB_PWIDTH), F32),
        ],
        compiler_params=_params(2),
        name="mlstm",
    )(mb, mb, mb, mb, mb, gates, conv_w, conv_b, gate_b, hnorm, tri)


LOG2E = 1.4426950408889634
SOFTPLUS_CAP = 64.0
SC_QTILE = 512


def _sc_kernel(q_ref, k_ref, v_ref, z_ref, tri_ref, y_ref, qm_ref, acc_ref, o_ref, zs_ref, cs_ref,
               sp_ref, wb_ref, *, tile, nsub):
    qi = pl.program_id(1)
    npair = SC_WIDTH // LANE
    nch = nsub * SC_HEADS
    low = lax.broadcasted_iota(jnp.int32, (tile, LANE), 1) < SC_HEAD
    for u in range(nsub):
        for p in range(npair):
            q = q_ref[0, u * tile:(u + 1) * tile, p * LANE:(p + 1) * LANE]
            zero = jnp.zeros_like(q)
            qm_ref[u * SC_HEADS + 2 * p] = jnp.where(low, q, zero)
            qm_ref[u * SC_HEADS + 2 * p + 1] = jnp.where(low, zero, q)
    acc_ref[...] = jnp.zeros_like(acc_ref)
    o_ref[...] = jnp.zeros_like(o_ref)

    def chains_of(p, modes):
        ch = [(u, hh) for u in range(nsub) if modes[u] is not None for hh in range(2)]
        return ch, [u * SC_HEADS + 2 * p + hh for (u, hh) in ch]

    def strict_mask(modes):
        if 'diag' not in modes:
            return None
        t_idx = lax.broadcasted_iota(jnp.int32, (tile, tile), 0)
        s_idx = lax.broadcasted_iota(jnp.int32, (tile, tile), 1)
        return s_idx < t_idx

    def stage_logits(j, modes, slot):
        start = pl.multiple_of(j * tile, tile)
        strict = strict_mask(modes)
        for p in range(npair):
            kt = k_ref[0, pl.ds(start, tile), p * LANE:(p + 1) * LANE]
            chains, idx = chains_of(p, modes)
            for c in idx:
                zs_ref[slot * nch + c] = _dot_nt(qm_ref[c], kt)
            for c, (u, hh) in zip(idx, chains):
                z = zs_ref[slot * nch + c]
                sp = jnp.maximum(z, jnp.log(1.0 + jnp.exp2(jnp.minimum(z, SOFTPLUS_CAP))) * LOG2E)
                if modes[u] == 'diag':
                    sp = jnp.where(strict, sp, 0.0)
                sp_ref[slot * nch + c] = sp.astype(BF16)

    def stage_weights(j, modes, slot):
        start = pl.multiple_of(j * tile, tile)
        strict = strict_mask(modes)
        for p in range(npair):
            cols = slice(p * LANE, (p + 1) * LANE)
            vt = v_ref[0, pl.ds(start, tile), cols]
            zero_v = jnp.zeros_like(vt)
            vm = (jnp.where(low, vt, zero_v), jnp.where(low, zero_v, vt))
            chains, idx = chains_of(p, modes)
            for c in idx:
                cs_ref[c] = _dot(sp_ref[slot * nch + c], tri_ref[...])
            for c, (u, hh) in zip(idx, chains):
                cs = cs_ref[c]
                acc = acc_ref[c]
                w = jnp.exp2(zs_ref[slot * nch + c] - cs - jnp.concatenate([acc] * (tile // LANE), axis=1))
                if modes[u] == 'diag':
                    w = jnp.where(strict, w, 0.0)
                acc_ref[c] = acc + jnp.broadcast_to(cs[:, 0:1], acc.shape)
                wb_ref[c] = w.astype(BF16)
            for c, (u, hh) in zip(idx, chains):
                o_ref[u * tile:(u + 1) * tile, cols] += _dot(wb_ref[c], vm[hh])

    for c in reversed(range(nsub)):
        modes = [None if u < c else ('diag' if u == c else 'full') for u in range(nsub)]
        stage_logits(qi * nsub + c, modes, 0)
        stage_weights(qi * nsub + c, modes, 0)

    n = qi * nsub
    full = ['full'] * nsub

    @pl.when(n > 0)
    def _():
        stage_logits(n - 1, full, 0)

    def body(i, carry):
        stage_weights(n - i, full, (i - 1) % 2)
        stage_logits(n - 1 - i, full, i % 2)
        return carry

    lax.fori_loop(1, n, body, 0)

    @pl.when(n > 0)
    def _():
        stage_weights(0, full, (n - 1) % 2)

    y_ref[0] = (o_ref[...] * _silu(z_ref[0].astype(F32))).astype(BF16)


def _sc_call(sc, tri):
    b, s, _ = sc.shape
    tile = SC_TILE
    tq = SC_QTILE if s % SC_QTILE == 0 else tile
    nsub = tq // tile
    nch = nsub * SC_HEADS
    return pl.pallas_call(
        functools.partial(_sc_kernel, tile=tile, nsub=nsub),
        grid=(b, s // tq),
        in_specs=[
            pl.BlockSpec((1, tq, SC_WIDTH), lambda bi, qi: (bi, qi, 0)),
            pl.BlockSpec((1, s, SC_WIDTH), lambda bi, qi: (bi, 0, 1)),
            pl.BlockSpec((1, s, SC_WIDTH), lambda bi, qi: (bi, 0, 2)),
            pl.BlockSpec((1, tq, SC_WIDTH), lambda bi, qi: (bi, qi, 3)),
            pl.BlockSpec((tile, tile), lambda bi, qi: (0, 0)),
        ],
        out_specs=pl.BlockSpec((1, tq, SC_WIDTH), lambda bi, qi: (bi, qi, 0)),
        out_shape=jax.ShapeDtypeStruct((b, s, SC_WIDTH), BF16),
        scratch_shapes=[
            pltpu.VMEM((nch, tile, LANE), BF16),
            pltpu.VMEM((nch, tile, LANE), F32),
            pltpu.VMEM((tq, SC_WIDTH), F32),
            pltpu.VMEM((2 * nch, tile, tile), F32),
            pltpu.VMEM((nch, tile, tile), F32),
            pltpu.VMEM((2 * nch, tile, tile), BF16),
            pltpu.VMEM((nch, tile, tile), BF16),
        ],
        compiler_params=_params(2),
        name="stick_breaking",
    )(sc, sc, sc, sc, tri)


def _out_kernel(x_ref, mod_ref, ya_ref, yb_ref, yc_ref, wa_ref, wb_ref, wc_ref, o_ref,
                *, d_model):
    y = _dot(ya_ref[0], wa_ref[...]) + _dot(yb_ref[0], wb_ref[...]) + _dot(yc_ref[0], wc_ref[...])
    gate = mod_ref[0, :, 2 * d_model:3 * d_model]
    o_ref[0] = x_ref[0] + gate * y


def _out_call(x, mod3, ya, yb, yc, wa, wb, wc):
    b, s, d = x.shape
    tm = 1024 if s % 1024 == 0 else s
    row = lambda bi, si: (bi, si, 0)
    const2 = lambda bi, si: (0, 0)
    return pl.pallas_call(
        functools.partial(_out_kernel, d_model=d),
        grid=(b, s // tm),
        in_specs=[
            pl.BlockSpec((1, tm, d), row),
            pl.BlockSpec((1, 1, 3 * d), lambda bi, si: (bi, 0, 0)),
            pl.BlockSpec((1, tm, GA_WIDTH), row),
            pl.BlockSpec((1, tm, MB_PWIDTH), row),
            pl.BlockSpec((1, tm, SC_WIDTH), row),
            pl.BlockSpec((GA_WIDTH, d), const2),
            pl.BlockSpec((MB_PWIDTH, d), const2),
            pl.BlockSpec((SC_WIDTH, d), const2),
        ],
        out_specs=pl.BlockSpec((1, tm, d), row),
        out_shape=jax.ShapeDtypeStruct((b, s, d), F32),
        compiler_params=_params(2),
        name="out_proj",
    )(x, mod3, ya, yb, yc, wa, wb, wc)


def _pad_heads(a):
    lead = a.shape[:-1]
    a = a.reshape(lead + (MB_HEADS, MB_HEAD))
    a = jnp.pad(a, [(0, 0)] * len(lead) + [(0, 0), (0, MB_PAD - MB_HEAD)])
    return a.reshape(lead + (MB_PWIDTH,))


def _block_mean_matrix(width, group):
    idx = jnp.arange(width) // group
    return jnp.where(idx[:, None] == idx[None, :], 1.0 / group, 0.0).astype(BF16)


def _layer_params(w_in, ga_v_norm, ga_bs, mb_conv_w, mb_conv_b, mb_b_i, mb_b_f, mb_h_norm,
                  sc_q_norm, sc_k_norm, w_out):
    d = w_in.shape[0]
    o_mb = N_GA
    o_gate = o_mb + 5 * MB_WIDTH
    o_sc = o_gate + 2 * MB_HEADS
    w_mb = jnp.concatenate(
        [_pad_heads(w_in[:, o_mb + i * MB_WIDTH:o_mb + (i + 1) * MB_WIDTH]) for i in range(5)], axis=1)
    w_gate = jnp.pad(w_in[:, o_gate:o_sc], ((0, 0), (0, GATE_W - 2 * MB_HEADS)))
    w_r = jnp.concatenate([w_in[:, :N_GA], w_mb, w_gate, w_in[:, o_sc:]], axis=1).astype(BF16)

    conv_w = jnp.concatenate([_pad_heads(mb_conv_w[:, :MB_WIDTH]), _pad_heads(mb_conv_w[:, MB_WIDTH:])], axis=1)
    conv_b = jnp.concatenate([_pad_heads(mb_conv_b[:MB_WIDTH]), _pad_heads(mb_conv_b[MB_WIDTH:])])[None, :]
    gate_b = jnp.pad(jnp.concatenate([mb_b_i, mb_b_f]), (0, GATE_W - 2 * MB_HEADS))[None, :]
    hnorm = jnp.pad(mb_h_norm, ((0, 0), (0, MB_PAD - MB_HEAD)))
    hnorm = jnp.broadcast_to(hnorm[:, :, None], (MB_HEADS, MB_PAD, CHUNK))

    qkg = jnp.concatenate([jnp.tile(sc_q_norm, SC_HEADS) * (SC_HEAD ** -0.5 * LOG2E),
                           jnp.tile(sc_k_norm, SC_HEADS)])[None, :]
    vg = ga_v_norm.reshape(1, GA_WIDTH)
    bs_full = jnp.repeat(ga_bs.T, GA_HEAD, axis=1)

    wa = w_out[:GA_WIDTH].astype(BF16)
    wb = w_out[GA_WIDTH:GA_WIDTH + MB_WIDTH].reshape(MB_HEADS, MB_HEAD, d)
    wb = jnp.pad(wb, ((0, 0), (0, MB_PAD - MB_HEAD), (0, 0))).reshape(MB_PWIDTH, d).astype(BF16)
    wc = w_out[GA_WIDTH + MB_WIDTH:].astype(BF16)
    return dict(w_r=w_r, conv_w=conv_w, conv_b=conv_b, gate_b=gate_b, hnorm=hnorm, qkg=qkg,
                vg=vg, bs_full=bs_full, wa=wa, wb=wb, wc=wc)


def kernel(x, c, norm_g, w_ada, b_ada, w_in, ga_v_norm, ga_ws, ga_bs, mb_conv_w, mb_conv_b,
           mb_b_i, mb_b_f, mb_h_norm, sc_q_norm, sc_k_norm, w_out):
    depth = w_in.shape[0]
    b, s, d = x.shape
    assert s % SC_TILE == 0 and s % CHUNK == 0 and d % LANE == 0

    mod = _ada_call(c, w_ada, b_ada)
    gm_sc = _block_mean_matrix(2 * SC_WIDTH, SC_HEAD)
    gm_ga = _block_mean_matrix(GA_WIDTH, GA_HEAD)
    gm_ga = jnp.concatenate([gm_ga, gm_ga], axis=0)
    r = jnp.arange(CHUNK)
    tri_low = (r[None, :] <= r[:, None]).astype(F32)
    r2 = jnp.arange(SC_TILE)
    tri_suffix = (r2[:, None] >= r2[None, :]).astype(BF16)

    for l in range(depth):
        p = _layer_params(w_in[l], ga_v_norm[l], ga_bs[l], mb_conv_w[l], mb_conv_b[l], mb_b_i[l],
                          mb_b_f[l], mb_h_norm[l], sc_q_norm[l], sc_k_norm[l], w_out[l])
        mod3 = mod[l][:, None, :]
        ya, mb, gates, sc = _inproj_call(x, mod3, norm_g[l][None, :], p["w_r"], p["qkg"], gm_sc, p["vg"],
                                         ga_ws[l], p["bs_full"], gm_ga)
        yb = _mb_call(mb, gates, p["conv_w"], p["conv_b"], p["gate_b"], p["hnorm"], tri_low)
        yc = _sc_call(sc, tri_suffix)
        x = _out_call(x, mod3, ya, yb, yc, p["wa"], p["wb"], p["wc"])
    return x
```

```python
import functools

import jax
import jax.numpy as jnp
from jax import lax
from jax.experimental import pallas as pl
from jax.experimental.pallas import tpu as pltpu

CHUNK = 128
GA_GROUPS = 4
GA_HEAD = 64
GA_WIDTH = GA_GROUPS * GA_HEAD
MB_HEADS = 4
MB_HEAD = 96
MB_PAD = 128
MB_WIDTH = MB_HEADS * MB_HEAD
MB_PWIDTH = MB_HEADS * MB_PAD
MB_CONV = 4
SC_HEADS = 6
SC_HEAD = 64
SC_WIDTH = SC_HEADS * SC_HEAD
GATE_W = 128
EPS = 1e-6
LANE = 128
SC_TILE = 256
VMEM_LIMIT = 56 * 1024 * 1024

F32 = jnp.float32
BF16 = jnp.bfloat16
HIGHEST = lax.Precision.HIGHEST


def _dot(a, b):
    return jnp.dot(a, b, preferred_element_type=F32)


def _dot_nt(a, b):
    return lax.dot_general(a, b, (((1,), (1,)), ((), ())), preferred_element_type=F32)


def _split_dot(a, b2):
    hi = a.astype(BF16)
    lo = (a - hi.astype(F32)).astype(BF16)
    return _dot(jnp.concatenate([hi, lo], axis=1), b2)


def _sigmoid(x):
    return 1.0 / (1.0 + jnp.exp(-x))


def _silu(x):
    return x * _sigmoid(x)


def _gelu_tanh(x):
    return 0.5 * x * (1.0 + jnp.tanh(0.7978845608028654 * (x + 0.044715 * (x * x * x))))


def _params(n_axes):
    return pltpu.CompilerParams(
        dimension_semantics=("arbitrary",) * n_axes, vmem_limit_bytes=VMEM_LIMIT)


def _ada_kernel(c_ref, w_ref, b_ref, o_ref):
    c = c_ref[...]
    o_ref[0] = jnp.dot(_silu(c), w_ref[0], precision=HIGHEST,
                       preferred_element_type=F32) + b_ref[0]


def _ada_call(c, w_ada, b_ada):
    depth, d, d3 = w_ada.shape
    b = c.shape[0]
    tn = 1024 if d3 % 1024 == 0 else d3
    return pl.pallas_call(
        _ada_kernel,
        grid=(depth, d3 // tn),
        in_specs=[
            pl.BlockSpec((b, d), lambda l, n: (0, 0)),
            pl.BlockSpec((1, d, tn), lambda l, n: (l, 0, n)),
            pl.BlockSpec((1, 1, tn), lambda l, n: (l, 0, n)),
        ],
        out_specs=pl.BlockSpec((1, b, tn), lambda l, n: (l, 0, n)),
        out_shape=jax.ShapeDtypeStruct((depth, b, d3), F32),
        compiler_params=_params(2),
        name="ada_mod",
    )(c, w_ada, b_ada.reshape(depth, 1, d3))


N_GA = 3 * GA_WIDTH
N_MB = 5 * MB_PWIDTH
N_SC = 4 * SC_WIDTH
OFF_MB = N_GA
OFF_GATE = OFF_MB + N_MB
OFF_SC = OFF_GATE + GATE_W
N_PROJ = OFF_SC + N_SC


def _inproj_kernel(x_ref, mod_ref, g_ref, w_ref, qkg_ref, gm_ref, vg_ref, ws_ref, bs_ref, gmga_ref,
                   ya_ref, mb_ref, gate_ref, sc_ref, *, d_model, tm):
    x = x_ref[0]
    ms = jnp.mean(x * x, axis=-1, keepdims=True)
    shift = mod_ref[0, :, 0:d_model]
    scale = mod_ref[0, :, d_model:2 * d_model]
    gain = g_ref[...] * (1.0 + scale)
    hb = (x * lax.rsqrt(ms + EPS) * gain + shift).astype(BF16)

    def proj(c0, n):
        return _dot(hb, w_ref[:, c0:c0 + n])

    for c0 in range(0, N_MB, MB_PWIDTH):
        mb_ref[0, :, c0:c0 + MB_PWIDTH] = proj(OFF_MB + c0, MB_PWIDTH).astype(BF16)
    gate_ref[0] = proj(OFF_GATE, GATE_W)

    qk = proj(OFF_SC, 2 * SC_WIDTH)
    msq = _dot((qk * qk).astype(BF16), gm_ref[...])
    sc_ref[0, :, 0:2 * SC_WIDTH] = (qk * lax.rsqrt(msq + EPS) * qkg_ref[...]).astype(BF16)
    for c0 in range(2 * SC_WIDTH, N_SC, SC_WIDTH):
        sc_ref[0, :, c0:c0 + SC_WIDTH] = proj(OFF_SC + c0, SC_WIDTH).astype(BF16)

    u_all = _gelu_tanh(proj(0, GA_WIDTH))
    v_all = _gelu_tanh(proj(GA_WIDTH, GA_WIDTH))
    z_all = proj(2 * GA_WIDTH, GA_WIDTH)
    t_idx = lax.broadcasted_iota(jnp.int32, (CHUNK, CHUNK), 0)
    s_idx = lax.broadcasted_iota(jnp.int32, (CHUNK, CHUNK), 1)
    lane = lax.broadcasted_iota(jnp.int32, (CHUNK, GA_WIDTH), 1)
    w_causal = [jnp.where(s_idx <= t_idx, ws_ref[g], 0.0).astype(BF16) for g in range(GA_GROUPS)]
    for c in range(tm // CHUNK):
        rows = slice(c * CHUNK, (c + 1) * CHUNK)
        v = v_all[rows]
        msq_v = _split_dot(v * v, gmga_ref[...])
        vn = (v * lax.rsqrt(msq_v + EPS) * vg_ref[...]).astype(BF16)
        sp = bs_ref[...]
        for g in range(GA_GROUPS):
            in_group = (lane >= g * GA_HEAD) & (lane < (g + 1) * GA_HEAD)
            sp = sp + _dot(w_causal[g], jnp.where(in_group, vn, jnp.zeros_like(vn)))
        ya_ref[0, rows, :] = (u_all[rows] * sp * _silu(z_all[rows])).astype(BF16)


def _inproj_call(x, l, mod4, p, gmat, ws, gm_ga):
    b, s, d = x.shape
    tm = 512 if s % 512 == 0 else s
    assert tm % CHUNK == 0
    row = lambda bi, si: (bi, si, 0)
    const2 = lambda bi, si: (0, 0)
    layer2 = lambda bi, si: (l, 0, 0)
    return pl.pallas_call(
        functools.partial(_inproj_kernel, d_model=d, tm=tm),
        grid=(b, s // tm),
        in_specs=[
            pl.BlockSpec((1, tm, d), row),
            pl.BlockSpec((None, 1, 1, 3 * d), lambda bi, si: (l, bi, 0, 0)),
            pl.BlockSpec((None, 1, d), layer2),
            pl.BlockSpec((None, d, N_PROJ), layer2),
            pl.BlockSpec((None, 1, 2 * SC_WIDTH), layer2),
            pl.BlockSpec((2 * SC_WIDTH, 2 * SC_WIDTH), const2),
            pl.BlockSpec((None, 1, GA_WIDTH), layer2),
            pl.BlockSpec((None, GA_GROUPS, CHUNK, CHUNK), lambda bi, si: (l, 0, 0, 0)),
            pl.BlockSpec((None, CHUNK, GA_WIDTH), layer2),
            pl.BlockSpec((2 * GA_WIDTH, GA_WIDTH), const2),
        ],
        out_specs=[
            pl.BlockSpec((1, tm, GA_WIDTH), row),
            pl.BlockSpec((1, tm, N_MB), row),
            pl.BlockSpec((1, tm, GATE_W), row),
            pl.BlockSpec((1, tm, N_SC), row),
        ],
        out_shape=[
            jax.ShapeDtypeStruct((b, s, GA_WIDTH), BF16),
            jax.ShapeDtypeStruct((b, s, N_MB), BF16),
            jax.ShapeDtypeStruct((b, s, GATE_W), F32),
            jax.ShapeDtypeStruct((b, s, N_SC), BF16),
        ],
        compiler_params=_params(2),
        name="in_proj",
    )(x, mod4, p["norm_g"], p["w_r"], p["qkg"], gmat, p["vg"], ws, p["bs_full"], gm_ga)


MB_NB = 4


def _mb_kernel(q_ref, k_ref, v_ref, o_ref, z_ref, gate_ref, cw_ref, cb_ref, gb_ref,
               hn_ref, tri_ref, y_ref, c_state, m_state, ext_ref, *, nb):
    ci = pl.program_id(1)

    @pl.when(ci == 0)
    def _():
        c_state[...] = jnp.zeros_like(c_state)
        m_state[...] = jnp.zeros_like(m_state)
        ext_ref[:, 0:8, :] = jnp.zeros((nb, 8, 2 * MB_PWIDTH), F32)

    row = lax.broadcasted_iota(jnp.int32, (CHUNK, CHUNK), 0)
    col = lax.broadcasted_iota(jnp.int32, (CHUNK, CHUNK), 1)
    causal_t = row <= col

    for bi in range(nb):
        qk_raw = jnp.concatenate([q_ref[bi], k_ref[bi]], axis=1).astype(F32)
        ext_ref[bi, 8:8 + CHUNK, :] = qk_raw
        conv = cb_ref[...] + cw_ref[MB_CONV - 1:MB_CONV, :] * qk_raw
        for back in range(1, MB_CONV):
            tap = MB_CONV - 1 - back
            conv = conv + cw_ref[tap:tap + 1, :] * ext_ref[bi, 8 - back:8 - back + CHUNK, :]
        ext_ref[bi, 0:8, :] = qk_raw[CHUNK - 8:CHUNK, :]
        qk = _silu(conv)
        q_all = qk[:, 0:MB_PWIDTH].astype(BF16)
        k_all = (qk[:, MB_PWIDTH:2 * MB_PWIDTH] * (MB_HEAD ** -0.5)).astype(BF16)

        graw = gate_ref[bi] + gb_ref[...]
        lane_g = lax.broadcasted_iota(jnp.int32, graw.shape, 1)
        log_f = jnp.minimum(graw, 0.0) - jnp.log1p(jnp.exp(-jnp.abs(graw)))
        gm = jnp.where((lane_g >= MB_HEADS) & (lane_g < 2 * MB_HEADS), log_f, graw)
        bc = jnp.dot(tri_ref[...], gm, precision=HIGHEST, preferred_element_type=F32)
        gm_t = gm.T
        bc_t = bc.T

        for h in range(MB_HEADS):
            cs = slice(h * MB_PAD, (h + 1) * MB_PAD)
            b_row = bc_t[MB_HEADS + h:MB_HEADS + h + 1, :]
            a_row = gm_t[h:h + 1, :] - b_row
            a_col = gm[:, h:h + 1] - bc[:, MB_HEADS + h:MB_HEADS + h + 1]
            m_prev = m_state[bi, h:h + 1, :]

            d_log = jnp.where(causal_t, b_row + a_col, -jnp.inf)
            inter = b_row + m_prev
            m_t = jnp.maximum(inter, jnp.max(d_log, axis=0, keepdims=True))
            w_intra = jnp.exp(d_log - m_t)
            w_inter = jnp.exp(inter - m_t)

            qh = q_all[:, cs]
            kh = k_all[:, cs]
            v_t = v_ref[bi, :, cs].astype(F32).T
            v_aug = jnp.where(row == MB_HEAD, 1.0, v_t)
            s_mat = _dot_nt(kh, qh) * w_intra
            num = _dot(v_aug.astype(BF16), s_mat.astype(BF16)) \
                + w_inter * _dot_nt(c_state[bi, h].astype(BF16), qh)
            den = num[MB_HEAD:MB_HEAD + 1, :]
            hh = num * (1.0 / jnp.maximum(jnp.abs(den), jnp.exp(-m_t)))
            hh = jnp.where(row < MB_HEAD, hh, 0.0)

            b_tot = b_row[:, CHUNK - 1:CHUNK]
            m_new = m_t[:, CHUNK - 1:CHUNK]
            decay = jnp.exp(b_tot + m_prev[:, 0:1] - m_new)
            ws_row = jnp.exp(b_tot + a_row - m_new)
            c_state[bi, h] = decay * c_state[bi, h] + _dot((v_aug * ws_row).astype(BF16), kh)
            m_state[bi, h:h + 1, :] = jnp.broadcast_to(m_new, (1, LANE))

            msq = jnp.sum(hh * hh, axis=0, keepdims=True) * (1.0 / MB_HEAD)
            hn = (hh * lax.rsqrt(msq + EPS) * hn_ref[h]).T
            y = _sigmoid(o_ref[bi, :, cs].astype(F32)) * hn * _silu(z_ref[bi, :, cs].astype(F32))
            y_ref[bi, :, cs] = y.astype(BF16)


def _mb_call(mb, gates, l, p, tri):
    b, s, _ = mb.shape
    nc = s // CHUNK
    nb = MB_NB if b % MB_NB == 0 else 1
    col = lambda j: (lambda bi, ci: (bi, ci, j))
    const2 = lambda bi, ci: (0, 0)
    layer2 = lambda bi, ci: (l, 0, 0)
    return pl.pallas_call(
        functools.partial(_mb_kernel, nb=nb),
        grid=(b // nb, nc),
        in_specs=[
            pl.BlockSpec((nb, CHUNK, MB_PWIDTH), col(0)),
            pl.BlockSpec((nb, CHUNK, MB_PWIDTH), col(1)),
            pl.BlockSpec((nb, CHUNK, MB_PWIDTH), col(2)),
            pl.BlockSpec((nb, CHUNK, MB_PWIDTH), col(3)),
            pl.BlockSpec((nb, CHUNK, MB_PWIDTH), col(4)),
            pl.BlockSpec((nb, CHUNK, GATE_W), col(0)),
            pl.BlockSpec((None, MB_CONV, 2 * MB_PWIDTH), layer2),
            pl.BlockSpec((None, 1, 2 * MB_PWIDTH), layer2),
            pl.BlockSpec((None, 1, GATE_W), layer2),
            pl.BlockSpec((None, MB_HEADS, MB_PAD, CHUNK), lambda bi, ci: (l, 0, 0, 0)),
            pl.BlockSpec((CHUNK, CHUNK), const2),
        ],
        out_specs=pl.BlockSpec((nb, CHUNK, MB_PWIDTH), col(0)),
        out_shape=jax.ShapeDtypeStruct((b, s, MB_PWIDTH), BF16),
        scratch_shapes=[
            pltpu.VMEM((nb, MB_HEADS, MB_PAD, MB_PAD), F32),
            pltpu.VMEM((nb, 8, LANE), F32),
            pltpu.VMEM((nb, 8 + CHUNK, 2 * MB_PWIDTH), F32),
        ],
        compiler_params=_params(2),
        name="mlstm",
    )(mb, mb, mb, mb, mb, gates, p["conv_w"], p["conv_b"], p["gate_b"], p["hnorm"], tri)


LOG2E = 1.4426950408889634
SOFTPLUS_CAP = 64.0
SC_QTILE = 1024


def _sc_kernel(q_ref, k_ref, v_ref, z_ref, tri_ref, y_ref, qm_ref, acc_ref, o_ref, zs_ref, cs_ref,
               sp_ref, wb_ref, *, tile, nsub):
    qi = pl.program_id(1)
    npair = SC_WIDTH // LANE
    nch = nsub * SC_HEADS
    low = lax.broadcasted_iota(jnp.int32, (tile, LANE), 1) < SC_HEAD
    for u in range(nsub):
        for p in range(npair):
            q = q_ref[0, u * tile:(u + 1) * tile, p * LANE:(p + 1) * LANE]
            zero = jnp.zeros_like(q)
            qm_ref[u * SC_HEADS + 2 * p] = jnp.where(low, q, zero)
            qm_ref[u * SC_HEADS + 2 * p + 1] = jnp.where(low, zero, q)
    acc_ref[...] = jnp.zeros_like(acc_ref)
    o_ref[...] = jnp.zeros_like(o_ref)

    def chains_of(p, modes):
        ch = [(u, hh) for u in range(nsub) if modes[u] is not None for hh in range(2)]
        return ch, [u * SC_HEADS + 2 * p + hh for (u, hh) in ch]

    def strict_mask(modes):
        if 'diag' not in modes:
            return None
        t_idx = lax.broadcasted_iota(jnp.int32, (tile, tile), 0)
        s_idx = lax.broadcasted_iota(jnp.int32, (tile, tile), 1)
        return s_idx < t_idx

    def stage_logits(j, modes, slot):
        start = pl.multiple_of(j * tile, tile)
        strict = strict_mask(modes)
        for p in range(npair):
            kt = k_ref[0, pl.ds(start, tile), p * LANE:(p + 1) * LANE]
            chains, idx = chains_of(p, modes)
            for c in idx:
                zs_ref[slot * nch + c] = _dot_nt(qm_ref[c], kt)
            for c, (u, hh) in zip(idx, chains):
                z = zs_ref[slot * nch + c]
                sp = jnp.maximum(z, jnp.log(1.0 + jnp.exp2(jnp.minimum(z, SOFTPLUS_CAP))) * LOG2E)
                if modes[u] == 'diag':
                    sp = jnp.where(strict, sp, 0.0)
                sp_ref[slot * nch + c] = sp.astype(BF16)

    def stage_weights(j, modes, slot):
        start = pl.multiple_of(j * tile, tile)
        strict = strict_mask(modes)
        for p in range(npair):
            cols = slice(p * LANE, (p + 1) * LANE)
            vt = v_ref[0, pl.ds(start, tile), cols]
            zero_v = jnp.zeros_like(vt)
            vm = (jnp.where(low, vt, zero_v), jnp.where(low, zero_v, vt))
            chains, idx = chains_of(p, modes)
            for c in idx:
                cs_ref[c] = _dot(sp_ref[slot * nch + c], tri_ref[...])
            for c, (u, hh) in zip(idx, chains):
                cs = cs_ref[c]
                acc = acc_ref[c]
                w = jnp.exp2(zs_ref[slot * nch + c] - cs - jnp.concatenate([acc] * (tile // LANE), axis=1))
                if modes[u] == 'diag':
                    w = jnp.where(strict, w, 0.0)
                acc_ref[c] = acc + jnp.broadcast_to(cs[:, 0:1], acc.shape)
                wb_ref[c] = w.astype(BF16)
            for c, (u, hh) in zip(idx, chains):
                o_ref[u * tile:(u + 1) * tile, cols] += _dot(wb_ref[c], vm[hh])

    for c in reversed(range(nsub)):
        modes = [None if u < c else ('diag' if u == c else 'full') for u in range(nsub)]
        stage_logits(qi * nsub + c, modes, 0)
        stage_weights(qi * nsub + c, modes, 0)

    n = qi * nsub
    full = ['full'] * nsub

    @pl.when(n > 0)
    def _():
        stage_logits(n - 1, full, 0)

    def body(i, carry):
        stage_weights(n - i, full, (i - 1) % 2)
        stage_logits(n - 1 - i, full, i % 2)
        return carry

    lax.fori_loop(1, n, body, 0)

    @pl.when(n > 0)
    def _():
        stage_weights(0, full, (n - 1) % 2)

    y_ref[0] = (o_ref[...] * _silu(z_ref[0].astype(F32))).astype(BF16)


def _sc_call(sc, tri):
    b, s, _ = sc.shape
    tile = SC_TILE
    tq = SC_QTILE if s % SC_QTILE == 0 else tile
    nsub = tq // tile
    nch = nsub * SC_HEADS
    return pl.pallas_call(
        functools.partial(_sc_kernel, tile=tile, nsub=nsub),
        grid=(b, s // tq),
        in_specs=[
            pl.BlockSpec((1, tq, SC_WIDTH), lambda bi, qi: (bi, qi, 0)),
            pl.BlockSpec((1, s, SC_WIDTH), lambda bi, qi: (bi, 0, 1)),
            pl.BlockSpec((1, s, SC_WIDTH), lambda bi, qi: (bi, 0, 2)),
            pl.BlockSpec((1, tq, SC_WIDTH), lambda bi, qi: (bi, qi, 3)),
            pl.BlockSpec((tile, tile), lambda bi, qi: (0, 0)),
        ],
        out_specs=pl.BlockSpec((1, tq, SC_WIDTH), lambda bi, qi: (bi, qi, 0)),
        out_shape=jax.ShapeDtypeStruct((b, s, SC_WIDTH), BF16),
        scratch_shapes=[
            pltpu.VMEM((nch, tile, LANE), BF16),
            pltpu.VMEM((nch, tile, LANE), F32),
            pltpu.VMEM((tq, SC_WIDTH), F32),
            pltpu.VMEM((2 * nch, tile, tile), F32),
            pltpu.VMEM((nch, tile, tile), F32),
            pltpu.VMEM((2 * nch, tile, tile), BF16),
            pltpu.VMEM((nch, tile, tile), BF16),
        ],
        compiler_params=_params(2),
        name="stick_breaking",
    )(sc, sc, sc, sc, tri)


def _out_kernel(x_ref, mod_ref, ya_ref, yb_ref, yc_ref, wa_ref, wb_ref, wc_ref, o_ref,
                *, d_model):
    y = _dot(ya_ref[0], wa_ref[...]) + _dot(yb_ref[0], wb_ref[...]) + _dot(yc_ref[0], wc_ref[...])
    gate = mod_ref[0, :, 2 * d_model:3 * d_model]
    o_ref[0] = x_ref[0] + gate * y


def _out_call(x, l, mod4, ya, yb, yc, p):
    b, s, d = x.shape
    tm = 1024 if s % 1024 == 0 else s
    row = lambda bi, si: (bi, si, 0)
    layer2 = lambda bi, si: (l, 0, 0)
    return pl.pallas_call(
        functools.partial(_out_kernel, d_model=d),
        grid=(b, s // tm),
        in_specs=[
            pl.BlockSpec((1, tm, d), row),
            pl.BlockSpec((None, 1, 1, 3 * d), lambda bi, si: (l, bi, 0, 0)),
            pl.BlockSpec((1, tm, GA_WIDTH), row),
            pl.BlockSpec((1, tm, MB_PWIDTH), row),
            pl.BlockSpec((1, tm, SC_WIDTH), row),
            pl.BlockSpec((None, GA_WIDTH, d), layer2),
            pl.BlockSpec((None, MB_PWIDTH, d), layer2),
            pl.BlockSpec((None, SC_WIDTH, d), layer2),
        ],
        out_specs=pl.BlockSpec((1, tm, d), row),
        out_shape=jax.ShapeDtypeStruct((b, s, d), F32),
        compiler_params=_params(2),
        name="out_proj",
    )(x, mod4, ya, yb, yc, p["wa"], p["wb"], p["wc"])


def _pad_heads(a):
    lead = a.shape[:-1]
    a = a.reshape(lead + (MB_HEADS, MB_HEAD))
    a = jnp.pad(a, [(0, 0)] * len(lead) + [(0, 0), (0, MB_PAD - MB_HEAD)])
    return a.reshape(lead + (MB_PWIDTH,))


def _block_mean_matrix(width, group):
    idx = jnp.arange(width) // group
    return jnp.where(idx[:, None] == idx[None, :], 1.0 / group, 0.0).astype(BF16)


def _stacked_params(norm_g, w_in, ga_v_norm, ga_bs, mb_conv_w, mb_conv_b, mb_b_i, mb_b_f, mb_h_norm,
                    sc_q_norm, sc_k_norm, w_out):
    depth, d, _ = w_in.shape
    o_mb = N_GA
    o_gate = o_mb + 5 * MB_WIDTH
    o_sc = o_gate + 2 * MB_HEADS
    w_mb = _pad_heads(w_in[:, :, o_mb:o_gate].reshape(depth, d, 5, MB_WIDTH)).reshape(depth, d, N_MB)
    w_gate = jnp.pad(w_in[:, :, o_gate:o_sc], ((0, 0), (0, 0), (0, GATE_W - 2 * MB_HEADS)))
    w_r = jnp.concatenate([w_in[:, :, :N_GA], w_mb, w_gate, w_in[:, :, o_sc:]], axis=2).astype(BF16)

    conv_w = _pad_heads(mb_conv_w.reshape(depth, MB_CONV, 2, MB_WIDTH)).reshape(depth, MB_CONV, 2 * MB_PWIDTH)
    conv_b = _pad_heads(mb_conv_b.reshape(depth, 1, 2, MB_WIDTH)).reshape(depth, 1, 2 * MB_PWIDTH)
    gate_b = jnp.pad(jnp.concatenate([mb_b_i, mb_b_f], axis=1), ((0, 0), (0, GATE_W - 2 * MB_HEADS)))[:, None, :]
    hnorm = jnp.pad(mb_h_norm, ((0, 0), (0, 0), (0, MB_PAD - MB_HEAD)))
    hnorm = jnp.broadcast_to(hnorm[:, :, :, None], (depth, MB_HEADS, MB_PAD, CHUNK))

    qkg = jnp.concatenate([jnp.tile(sc_q_norm, (1, SC_HEADS)) * (SC_HEAD ** -0.5 * LOG2E),
                           jnp.tile(sc_k_norm, (1, SC_HEADS))], axis=1)[:, None, :]
    vg = ga_v_norm.reshape(depth, 1, GA_WIDTH)
    bs_full = jnp.repeat(jnp.swapaxes(ga_bs, 1, 2), GA_HEAD, axis=2)

    wa = w_out[:, :GA_WIDTH].astype(BF16)
    wb = w_out[:, GA_WIDTH:GA_WIDTH + MB_WIDTH].reshape(depth, MB_HEADS, MB_HEAD, d)
    wb = jnp.pad(wb, ((0, 0), (0, 0), (0, MB_PAD - MB_HEAD), (0, 0))).reshape(depth, MB_PWIDTH, d).astype(BF16)
    wc = w_out[:, GA_WIDTH + MB_WIDTH:].astype(BF16)
    return dict(norm_g=norm_g[:, None, :], w_r=w_r, conv_w=conv_w, conv_b=conv_b, gate_b=gate_b, hnorm=hnorm,
                qkg=qkg, vg=vg, bs_full=bs_full, wa=wa, wb=wb, wc=wc)


def kernel(x, c, norm_g, w_ada, b_ada, w_in, ga_v_norm, ga_ws, ga_bs, mb_conv_w, mb_conv_b,
           mb_b_i, mb_b_f, mb_h_norm, sc_q_norm, sc_k_norm, w_out):
    depth = w_in.shape[0]
    b, s, d = x.shape
    assert s % SC_TILE == 0 and s % CHUNK == 0 and d % LANE == 0

    mod4 = _ada_call(c, w_ada, b_ada)[:, :, None, :]
    p = _stacked_params(norm_g, w_in, ga_v_norm, ga_bs, mb_conv_w, mb_conv_b, mb_b_i, mb_b_f, mb_h_norm,
                        sc_q_norm, sc_k_norm, w_out)
    gm_sc = _block_mean_matrix(2 * SC_WIDTH, SC_HEAD)
    gm_ga = _block_mean_matrix(GA_WIDTH, GA_HEAD)
    gm_ga = jnp.concatenate([gm_ga, gm_ga], axis=0)
    r = jnp.arange(CHUNK)
    tri_low = (r[None, :] <= r[:, None]).astype(F32)
    r2 = jnp.arange(SC_TILE)
    tri_suffix = (r2[:, None] >= r2[None, :]).astype(BF16)

    for l in range(depth):
        ya, mb, gates, sc = _inproj_call(x, l, mod4, p, gm_sc, ga_ws, gm_ga)
        yb = _mb_call(mb, gates, l, p, tri_low)
        yc = _sc_call(sc, tri_suffix)
        x = _out_call(x, l, mod4, ya, yb, yc, p)
    return x
```

```python
import functools

import jax
import jax.numpy as jnp
from jax import lax
from jax.experimental import pallas as pl
from jax.experimental.pallas import tpu as pltpu

CHUNK = 128
GA_GROUPS = 4
GA_HEAD = 64
GA_WIDTH = GA_GROUPS * GA_HEAD
MB_HEADS = 4
MB_HEAD = 96
MB_PAD = 128
MB_WIDTH = MB_HEADS * MB_HEAD
MB_PWIDTH = MB_HEADS * MB_PAD
MB_CONV = 4
SC_HEADS = 6
SC_HEAD = 64
SC_WIDTH = SC_HEADS * SC_HEAD
GATE_W = 128
EPS = 1e-6
LANE = 128
SC_TILE = 256
VMEM_LIMIT = 56 * 1024 * 1024

F32 = jnp.float32
BF16 = jnp.bfloat16
HIGHEST = lax.Precision.HIGHEST


def _dot(a, b):
    return jnp.dot(a, b, preferred_element_type=F32)


def _dot_nt(a, b):
    return lax.dot_general(a, b, (((1,), (1,)), ((), ())), preferred_element_type=F32)


def _sigmoid(x):
    return 1.0 / (1.0 + jnp.exp(-x))


def _silu(x):
    return x * _sigmoid(x)


def _gelu_tanh(x):
    return 0.5 * x * (1.0 + jnp.tanh(0.7978845608028654 * (x + 0.044715 * (x * x * x))))


def _params(n_axes):
    return pltpu.CompilerParams(
        dimension_semantics=("arbitrary",) * n_axes, vmem_limit_bytes=VMEM_LIMIT)


def _ada_kernel(c_ref, w_ref, b_ref, o_ref):
    c = c_ref[...]
    o_ref[0] = jnp.dot(_silu(c), w_ref[0], precision=HIGHEST,
                       preferred_element_type=F32) + b_ref[0]


def _ada_call(c, w_ada, b_ada):
    depth, d, d3 = w_ada.shape
    b = c.shape[0]
    tn = 1024 if d3 % 1024 == 0 else d3
    return pl.pallas_call(
        _ada_kernel,
        grid=(depth, d3 // tn),
        in_specs=[
            pl.BlockSpec((b, d), lambda l, n: (0, 0)),
            pl.BlockSpec((1, d, tn), lambda l, n: (l, 0, n)),
            pl.BlockSpec((1, 1, tn), lambda l, n: (l, 0, n)),
        ],
        out_specs=pl.BlockSpec((1, b, tn), lambda l, n: (l, 0, n)),
        out_shape=jax.ShapeDtypeStruct((depth, b, d3), F32),
        compiler_params=_params(2),
        name="ada_mod",
    )(c, w_ada, b_ada.reshape(depth, 1, d3))


N_GA = 3 * GA_WIDTH
N_MB = 5 * MB_PWIDTH
N_SC = 4 * SC_WIDTH
OFF_MB = N_GA
OFF_GATE = OFF_MB + N_MB
OFF_SC = OFF_GATE + GATE_W
N_PROJ = OFF_SC + N_SC
NORM_BLOCK = 256


def _inproj_kernel(x_ref, mod_ref, g_ref, w_ref, qkg_ref, gm_ref, vg_ref, ws_ref, bs_ref, gmga_ref,
                   ya_ref, mb_ref, gate_ref, sc_ref, *, d_model, tm):
    x = x_ref[0]
    ms = jnp.mean(x * x, axis=-1, keepdims=True)
    shift = mod_ref[0, :, 0:d_model]
    scale = mod_ref[0, :, d_model:2 * d_model]
    gain = g_ref[...] * (1.0 + scale)
    hb = (x * lax.rsqrt(ms + EPS) * gain + shift).astype(BF16)

    def proj(c0, n):
        return _dot(hb, w_ref[:, c0:c0 + n])

    for c0 in range(0, N_MB, MB_PWIDTH):
        mb_ref[0, :, c0:c0 + MB_PWIDTH] = proj(OFF_MB + c0, MB_PWIDTH).astype(BF16)
    gate_ref[0] = proj(OFF_GATE, GATE_W)

    qk = proj(OFF_SC, 2 * SC_WIDTH)
    sq = (qk * qk).astype(BF16)
    msq = jnp.concatenate([_dot(sq[:, c0:c0 + NORM_BLOCK], gm_ref[...])
                           for c0 in range(0, 2 * SC_WIDTH, NORM_BLOCK)], axis=1)
    sc_ref[0, :, 0:2 * SC_WIDTH] = (qk * lax.rsqrt(msq + EPS) * qkg_ref[...]).astype(BF16)
    for c0 in range(2 * SC_WIDTH, N_SC, SC_WIDTH):
        sc_ref[0, :, c0:c0 + SC_WIDTH] = proj(OFF_SC + c0, SC_WIDTH).astype(BF16)

    u_all = _gelu_tanh(proj(0, GA_WIDTH))
    v_all = _gelu_tanh(proj(GA_WIDTH, GA_WIDTH))
    z_all = proj(2 * GA_WIDTH, GA_WIDTH)
    t_idx = lax.broadcasted_iota(jnp.int32, (CHUNK, CHUNK), 0)
    s_idx = lax.broadcasted_iota(jnp.int32, (CHUNK, CHUNK), 1)
    lane = lax.broadcasted_iota(jnp.int32, (CHUNK, GA_WIDTH), 1)
    w_causal = [jnp.where(s_idx <= t_idx, ws_ref[g], 0.0).astype(BF16) for g in range(GA_GROUPS)]
    for c in range(tm // CHUNK):
        rows = slice(c * CHUNK, (c + 1) * CHUNK)
        v = v_all[rows]
        msq_v = _dot((v * v).astype(BF16), gmga_ref[...])
        vn = (v * lax.rsqrt(msq_v + EPS) * vg_ref[...]).astype(BF16)
        sp = bs_ref[...]
        for g in range(GA_GROUPS):
            in_group = (lane >= g * GA_HEAD) & (lane < (g + 1) * GA_HEAD)
            sp = sp + _dot(w_causal[g], jnp.where(in_group, vn, jnp.zeros_like(vn)))
        ya_ref[0, rows, :] = (u_all[rows] * sp * _silu(z_all[rows])).astype(BF16)


def _inproj_call(x, l, mod4, p, gmat, ws, gm_ga):
    b, s, d = x.shape
    tm = 512 if s % 512 == 0 else s
    assert tm % CHUNK == 0
    row = lambda bi, si: (bi, si, 0)
    const2 = lambda bi, si: (0, 0)
    layer2 = lambda bi, si: (l, 0, 0)
    return pl.pallas_call(
        functools.partial(_inproj_kernel, d_model=d, tm=tm),
        grid=(b, s // tm),
        in_specs=[
            pl.BlockSpec((1, tm, d), row),
            pl.BlockSpec((None, 1, 1, 3 * d), lambda bi, si: (l, bi, 0, 0)),
            pl.BlockSpec((None, 1, d), layer2),
            pl.BlockSpec((None, d, N_PROJ), layer2),
            pl.BlockSpec((None, 1, 2 * SC_WIDTH), layer2),
            pl.BlockSpec((NORM_BLOCK, NORM_BLOCK), const2),
            pl.BlockSpec((None, 1, GA_WIDTH), layer2),
            pl.BlockSpec((None, GA_GROUPS, CHUNK, CHUNK), lambda bi, si: (l, 0, 0, 0)),
            pl.BlockSpec((None, CHUNK, GA_WIDTH), layer2),
            pl.BlockSpec((GA_WIDTH, GA_WIDTH), const2),
        ],
        out_specs=[
            pl.BlockSpec((1, tm, GA_WIDTH), row),
            pl.BlockSpec((1, tm, N_MB), row),
            pl.BlockSpec((1, tm, GATE_W), row),
            pl.BlockSpec((1, tm, N_SC), row),
        ],
        out_shape=[
            jax.ShapeDtypeStruct((b, s, GA_WIDTH), BF16),
            jax.ShapeDtypeStruct((b, s, N_MB), BF16),
            jax.ShapeDtypeStruct((b, s, GATE_W), F32),
            jax.ShapeDtypeStruct((b, s, N_SC), BF16),
        ],
        compiler_params=_params(2),
        name="in_proj",
    )(x, mod4, p["norm_g"], p["w_r"], p["qkg"], gmat, p["vg"], ws, p["bs_full"], gm_ga)


MB_NB = 4


def _mb_kernel(q_ref, k_ref, v_ref, o_ref, z_ref, gate_ref, cw_ref, cb_ref, gb_ref,
               hn_ref, tri_ref, y_ref, c_state, m_state, ext_ref, *, nb):
    ci = pl.program_id(1)

    @pl.when(ci == 0)
    def _():
        c_state[...] = jnp.zeros_like(c_state)
        m_state[...] = jnp.zeros_like(m_state)
        ext_ref[:, 0:8, :] = jnp.zeros((nb, 8, 2 * MB_PWIDTH), F32)

    row = lax.broadcasted_iota(jnp.int32, (CHUNK, CHUNK), 0)
    col = lax.broadcasted_iota(jnp.int32, (CHUNK, CHUNK), 1)
    causal_t = row <= col

    for bi in range(nb):
        qk_raw = jnp.concatenate([q_ref[bi], k_ref[bi]], axis=1).astype(F32)
        ext_ref[bi, 8:8 + CHUNK, :] = qk_raw
        conv = cb_ref[...] + cw_ref[MB_CONV - 1:MB_CONV, :] * qk_raw
        for back in range(1, MB_CONV):
            tap = MB_CONV - 1 - back
            conv = conv + cw_ref[tap:tap + 1, :] * ext_ref[bi, 8 - back:8 - back + CHUNK, :]
        ext_ref[bi, 0:8, :] = qk_raw[CHUNK - 8:CHUNK, :]
        qk = _silu(conv)
        q_all = qk[:, 0:MB_PWIDTH].astype(BF16)
        k_all = (qk[:, MB_PWIDTH:2 * MB_PWIDTH] * (MB_HEAD ** -0.5)).astype(BF16)

        graw = gate_ref[bi] + gb_ref[...]
        lane_g = lax.broadcasted_iota(jnp.int32, graw.shape, 1)
        log_f = jnp.minimum(graw, 0.0) - jnp.log1p(jnp.exp(-jnp.abs(graw)))
        gm = jnp.where((lane_g >= MB_HEADS) & (lane_g < 2 * MB_HEADS), log_f, graw)
        bc = jnp.dot(tri_ref[...], gm, precision=HIGHEST, preferred_element_type=F32)
        gm_t = gm.T
        bc_t = bc.T

        for h in range(MB_HEADS):
            cs = slice(h * MB_PAD, (h + 1) * MB_PAD)
            b_row = bc_t[MB_HEADS + h:MB_HEADS + h + 1, :]
            a_row = gm_t[h:h + 1, :] - b_row
            a_col = gm[:, h:h + 1] - bc[:, MB_HEADS + h:MB_HEADS + h + 1]
            m_prev = m_state[bi, h:h + 1, :]

            d_log = jnp.where(causal_t, b_row + a_col, -jnp.inf)
            inter = b_row + m_prev
            m_t = jnp.maximum(inter, jnp.max(d_log, axis=0, keepdims=True))
            w_intra = jnp.exp(d_log - m_t)
            w_inter = jnp.exp(inter - m_t)

            qh = q_all[:, cs]
            kh = k_all[:, cs]
            v_t = v_ref[bi, :, cs].astype(F32).T
            v_aug = jnp.where(row == MB_HEAD, 1.0, v_t)
            s_mat = _dot_nt(kh, qh) * w_intra
            num = _dot(v_aug.astype(BF16), s_mat.astype(BF16)) \
                + w_inter * _dot_nt(c_state[bi, h].astype(BF16), qh)
            den = num[MB_HEAD:MB_HEAD + 1, :]
            hh = num * (1.0 / jnp.maximum(jnp.abs(den), jnp.exp(-m_t)))
            hh = jnp.where(row < MB_HEAD, hh, 0.0)

            b_tot = b_row[:, CHUNK - 1:CHUNK]
            m_new = m_t[:, CHUNK - 1:CHUNK]
            decay = jnp.exp(b_tot + m_prev[:, 0:1] - m_new)
            ws_row = jnp.exp(b_tot + a_row - m_new)
            c_state[bi, h] = decay * c_state[bi, h] + _dot((v_aug * ws_row).astype(BF16), kh)
            m_state[bi, h:h + 1, :] = jnp.broadcast_to(m_new, (1, LANE))

            msq = jnp.sum(hh * hh, axis=0, keepdims=True) * (1.0 / MB_HEAD)
            hn = (hh * lax.rsqrt(msq + EPS) * hn_ref[h]).T
            y = _sigmoid(o_ref[bi, :, cs].astype(F32)) * hn * _silu(z_ref[bi, :, cs].astype(F32))
            y_ref[bi, :, cs] = y.astype(BF16)


def _mb_call(mb, gates, l, p, tri):
    b, s, _ = mb.shape
    nc = s // CHUNK
    nb = MB_NB if b % MB_NB == 0 else 1
    col = lambda j: (lambda bi, ci: (bi, ci, j))
    const2 = lambda bi, ci: (0, 0)
    layer2 = lambda bi, ci: (l, 0, 0)
    return pl.pallas_call(
        functools.partial(_mb_kernel, nb=nb),
        grid=(b // nb, nc),
        in_specs=[
            pl.BlockSpec((nb, CHUNK, MB_PWIDTH), col(0)),
            pl.BlockSpec((nb, CHUNK, MB_PWIDTH), col(1)),
            pl.BlockSpec((nb, CHUNK, MB_PWIDTH), col(2)),
            pl.BlockSpec((nb, CHUNK, MB_PWIDTH), col(3)),
            pl.BlockSpec((nb, CHUNK, MB_PWIDTH), col(4)),
            pl.BlockSpec((nb, CHUNK, GATE_W), col(0)),
            pl.BlockSpec((None, MB_CONV, 2 * MB_PWIDTH), layer2),
            pl.BlockSpec((None, 1, 2 * MB_PWIDTH), layer2),
            pl.BlockSpec((None, 1, GATE_W), layer2),
            pl.BlockSpec((None, MB_HEADS, MB_PAD, CHUNK), lambda bi, ci: (l, 0, 0, 0)),
            pl.BlockSpec((CHUNK, CHUNK), const2),
        ],
        out_specs=pl.BlockSpec((nb, CHUNK, MB_PWIDTH), col(0)),
        out_shape=jax.ShapeDtypeStruct((b, s, MB_PWIDTH), BF16),
        scratch_shapes=[
            pltpu.VMEM((nb, MB_HEADS, MB_PAD, MB_PAD), F32),
            pltpu.VMEM((nb, 8, LANE), F32),
            pltpu.VMEM((nb, 8 + CHUNK, 2 * MB_PWIDTH), F32),
        ],
        compiler_params=_params(2),
        name="mlstm",
    )(mb, mb, mb, mb, mb, gates, p["conv_w"], p["conv_b"], p["gate_b"], p["hnorm"], tri)


LOG2E = 1.4426950408889634
SOFTPLUS_CAP = 64.0
SC_QTILE = 1024


def _sc_kernel(q_ref, k_ref, v_ref, z_ref, tri_ref, y_ref, qm_ref, acc_ref, o_ref, zs_ref, cs_ref,
               sp_ref, wb_ref, *, tile, nsub):
    qi = pl.program_id(1)
    npair = SC_WIDTH // LANE
    nch = nsub * SC_HEADS
    low = lax.broadcasted_iota(jnp.int32, (tile, LANE), 1) < SC_HEAD
    for u in range(nsub):
        for p in range(npair):
            q = q_ref[0, u * tile:(u + 1) * tile, p * LANE:(p + 1) * LANE]
            zero = jnp.zeros_like(q)
            qm_ref[u * SC_HEADS + 2 * p] = jnp.where(low, q, zero)
            qm_ref[u * SC_HEADS + 2 * p + 1] = jnp.where(low, zero, q)
    acc_ref[...] = jnp.zeros_like(acc_ref)
    o_ref[...] = jnp.zeros_like(o_ref)

    def chains_of(p, modes):
        ch = [(u, hh) for u in range(nsub) if modes[u] is not None for hh in range(2)]
        return ch, [u * SC_HEADS + 2 * p + hh for (u, hh) in ch]

    def strict_mask(modes):
        if 'diag' not in modes:
            return None
        t_idx = lax.broadcasted_iota(jnp.int32, (tile, tile), 0)
        s_idx = lax.broadcasted_iota(jnp.int32, (tile, tile), 1)
        return s_idx < t_idx

    def stage_logits(j, modes, slot):
        start = pl.multiple_of(j * tile, tile)
        strict = strict_mask(modes)
        for p in range(npair):
            kt = k_ref[0, pl.ds(start, tile), p * LANE:(p + 1) * LANE]
            chains, idx = chains_of(p, modes)
            for c in idx:
                zs_ref[slot * nch + c] = _dot_nt(qm_ref[c], kt)
            for c, (u, hh) in zip(idx, chains):
                z = zs_ref[slot * nch + c]
                sp = jnp.maximum(z, jnp.log(1.0 + jnp.exp2(jnp.minimum(z, SOFTPLUS_CAP))) * LOG2E)
                if modes[u] == 'diag':
                    sp = jnp.where(strict, sp, 0.0)
                sp_ref[slot * nch + c] = sp.astype(BF16)

    def stage_weights(j, modes, slot):
        start = pl.multiple_of(j * tile, tile)
        strict = strict_mask(modes)
        for p in range(npair):
            cols = slice(p * LANE, (p + 1) * LANE)
            vt = v_ref[0, pl.ds(start, tile), cols]
            zero_v = jnp.zeros_like(vt)
            vm = (jnp.where(low, vt, zero_v), jnp.where(low, zero_v, vt))
            chains, idx = chains_of(p, modes)
            for c in idx:
                cs_ref[c] = _dot(sp_ref[slot * nch + c], tri_ref[...])
            for c, (u, hh) in zip(idx, chains):
                cs = cs_ref[c]
                acc = acc_ref[c]
                w = jnp.exp2(zs_ref[slot * nch + c] - cs - jnp.concatenate([acc] * (tile // LANE), axis=1))
                if modes[u] == 'diag':
                    w = jnp.where(strict, w, 0.0)
                acc_ref[c] = acc + jnp.broadcast_to(cs[:, 0:1], acc.shape)
                wb_ref[c] = w.astype(BF16)
            for c, (u, hh) in zip(idx, chains):
                o_ref[u * tile:(u + 1) * tile, cols] += _dot(wb_ref[c], vm[hh])

    for c in reversed(range(nsub)):
        modes = [None if u < c else ('diag' if u == c else 'full') for u in range(nsub)]
        stage_logits(qi * nsub + c, modes, 0)
        stage_weights(qi * nsub + c, modes, 0)

    n = qi * nsub
    full = ['full'] * nsub

    @pl.when(n > 0)
    def _():
        stage_logits(n - 1, full, 0)

    def body(i, carry):
        stage_weights(n - i, full, (i - 1) % 2)
        stage_logits(n - 1 - i, full, i % 2)
        return carry

    lax.fori_loop(1, n, body, 0)

    @pl.when(n > 0)
    def _():
        stage_weights(0, full, (n - 1) % 2)

    y_ref[0] = (o_ref[...] * _silu(z_ref[0].astype(F32))).astype(BF16)


def _sc_call(sc, tri):
    b, s, _ = sc.shape
    tile = SC_TILE
    tq = SC_QTILE if s % SC_QTILE == 0 else tile
    nsub = tq // tile
    nch = nsub * SC_HEADS
    return pl.pallas_call(
        functools.partial(_sc_kernel, tile=tile, nsub=nsub),
        grid=(b, s // tq),
        in_specs=[
            pl.BlockSpec((1, tq, SC_WIDTH), lambda bi, qi: (bi, qi, 0)),
            pl.BlockSpec((1, s, SC_WIDTH), lambda bi, qi: (bi, 0, 1)),
            pl.BlockSpec((1, s, SC_WIDTH), lambda bi, qi: (bi, 0, 2)),
            pl.BlockSpec((1, tq, SC_WIDTH), lambda bi, qi: (bi, qi, 3)),
            pl.BlockSpec((tile, tile), lambda bi, qi: (0, 0)),
        ],
        out_specs=pl.BlockSpec((1, tq, SC_WIDTH), lambda bi, qi: (bi, qi, 0)),
        out_shape=jax.ShapeDtypeStruct((b, s, SC_WIDTH), BF16),
        scratch_shapes=[
            pltpu.VMEM((nch, tile, LANE), BF16),
            pltpu.VMEM((nch, tile, LANE), F32),
            pltpu.VMEM((tq, SC_WIDTH), F32),
            pltpu.VMEM((2 * nch, tile, tile), F32),
            pltpu.VMEM((nch, tile, tile), F32),
            pltpu.VMEM((2 * nch, tile, tile), BF16),
            pltpu.VMEM((nch, tile, tile), BF16),
        ],
        compiler_params=_params(2),
        name="stick_breaking",
    )(sc, sc, sc, sc, tri)


def _out_kernel(x_ref, mod_ref, ya_ref, yb_ref, yc_ref, wa_ref, wb_ref, wc_ref, o_ref,
                *, d_model):
    y = _dot(ya_ref[0], wa_ref[...]) + _dot(yb_ref[0], wb_ref[...]) + _dot(yc_ref[0], wc_ref[...])
    gate = mod_ref[0, :, 2 * d_model:3 * d_model]
    o_ref[0] = x_ref[0] + gate * y


def _out_call(x, l, mod4, ya, yb, yc, p):
    b, s, d = x.shape
    tm = 1024 if s % 1024 == 0 else s
    row = lambda bi, si: (bi, si, 0)
    layer2 = lambda bi, si: (l, 0, 0)
    return pl.pallas_call(
        functools.partial(_out_kernel, d_model=d),
        grid=(b, s // tm),
        in_specs=[
            pl.BlockSpec((1, tm, d), row),
            pl.BlockSpec((None, 1, 1, 3 * d), lambda bi, si: (l, bi, 0, 0)),
            pl.BlockSpec((1, tm, GA_WIDTH), row),
            pl.BlockSpec((1, tm, MB_PWIDTH), row),
            pl.BlockSpec((1, tm, SC_WIDTH), row),
            pl.BlockSpec((None, GA_WIDTH, d), layer2),
            pl.BlockSpec((None, MB_PWIDTH, d), layer2),
            pl.BlockSpec((None, SC_WIDTH, d), layer2),
        ],
        out_specs=pl.BlockSpec((1, tm, d), row),
        out_shape=jax.ShapeDtypeStruct((b, s, d), F32),
        compiler_params=_params(2),
        name="out_proj",
    )(x, mod4, ya, yb, yc, p["wa"], p["wb"], p["wc"])


def _pad_heads(a):
    lead = a.shape[:-1]
    a = a.reshape(lead + (MB_HEADS, MB_HEAD))
    a = jnp.pad(a, [(0, 0)] * len(lead) + [(0, 0), (0, MB_PAD - MB_HEAD)])
    return a.reshape(lead + (MB_PWIDTH,))


def _block_mean_matrix(width, group):
    idx = jnp.arange(width) // group
    return jnp.where(idx[:, None] == idx[None, :], 1.0 / group, 0.0).astype(BF16)


def _stacked_params(norm_g, w_in, ga_v_norm, ga_bs, mb_conv_w, mb_conv_b, mb_b_i, mb_b_f, mb_h_norm,
                    sc_q_norm, sc_k_norm, w_out):
    depth, d, _ = w_in.shape
    o_mb = N_GA
    o_gate = o_mb + 5 * MB_WIDTH
    o_sc = o_gate + 2 * MB_HEADS
    w_mb = _pad_heads(w_in[:, :, o_mb:o_gate].reshape(depth, d, 5, MB_WIDTH)).reshape(depth, d, N_MB)
    w_gate = jnp.pad(w_in[:, :, o_gate:o_sc], ((0, 0), (0, 0), (0, GATE_W - 2 * MB_HEADS)))
    w_r = jnp.concatenate([w_in[:, :, :N_GA], w_mb, w_gate, w_in[:, :, o_sc:]], axis=2).astype(BF16)

    conv_w = _pad_heads(mb_conv_w.reshape(depth, MB_CONV, 2, MB_WIDTH)).reshape(depth, MB_CONV, 2 * MB_PWIDTH)
    conv_b = _pad_heads(mb_conv_b.reshape(depth, 1, 2, MB_WIDTH)).reshape(depth, 1, 2 * MB_PWIDTH)
    gate_b = jnp.pad(jnp.concatenate([mb_b_i, mb_b_f], axis=1), ((0, 0), (0, GATE_W - 2 * MB_HEADS)))[:, None, :]
    hnorm = jnp.pad(mb_h_norm, ((0, 0), (0, 0), (0, MB_PAD - MB_HEAD)))
    hnorm = jnp.broadcast_to(hnorm[:, :, :, None], (depth, MB_HEADS, MB_PAD, CHUNK))

    qkg = jnp.concatenate([jnp.tile(sc_q_norm, (1, SC_HEADS)) * (SC_HEAD ** -0.5 * LOG2E),
                           jnp.tile(sc_k_norm, (1, SC_HEADS))], axis=1)[:, None, :]
    vg = ga_v_norm.reshape(depth, 1, GA_WIDTH)
    bs_full = jnp.repeat(jnp.swapaxes(ga_bs, 1, 2), GA_HEAD, axis=2)

    wa = w_out[:, :GA_WIDTH].astype(BF16)
    wb = w_out[:, GA_WIDTH:GA_WIDTH + MB_WIDTH].reshape(depth, MB_HEADS, MB_HEAD, d)
    wb = jnp.pad(wb, ((0, 0), (0, 0), (0, MB_PAD - MB_HEAD), (0, 0))).reshape(depth, MB_PWIDTH, d).astype(BF16)
    wc = w_out[:, GA_WIDTH + MB_WIDTH:].astype(BF16)
    return dict(norm_g=norm_g[:, None, :], w_r=w_r, conv_w=conv_w, conv_b=conv_b, gate_b=gate_b, hnorm=hnorm,
                qkg=qkg, vg=vg, bs_full=bs_full, wa=wa, wb=wb, wc=wc)


def kernel(x, c, norm_g, w_ada, b_ada, w_in, ga_v_norm, ga_ws, ga_bs, mb_conv_w, mb_conv_b,
           mb_b_i, mb_b_f, mb_h_norm, sc_q_norm, sc_k_norm, w_out):
    depth = w_in.shape[0]
    b, s, d = x.shape
    assert s % SC_TILE == 0 and s % CHUNK == 0 and d % LANE == 0

    mod4 = _ada_call(c, w_ada, b_ada)[:, :, None, :]
    p = _stacked_params(norm_g, w_in, ga_v_norm, ga_bs, mb_conv_w, mb_conv_b, mb_b_i, mb_b_f, mb_h_norm,
                        sc_q_norm, sc_k_norm, w_out)
    gm_sc = _block_mean_matrix(NORM_BLOCK, SC_HEAD)
    gm_ga = _block_mean_matrix(GA_WIDTH, GA_HEAD)
    r = jnp.arange(CHUNK)
    tri_low = (r[None, :] <= r[:, None]).astype(F32)
    r2 = jnp.arange(SC_TILE)
    tri_suffix = (r2[:, None] >= r2[None, :]).astype(BF16)

    for l in range(depth):
        ya, mb, gates, sc = _inproj_call(x, l, mod4, p, gm_sc, ga_ws, gm_ga)
        yb = _mb_call(mb, gates, l, p, tri_low)
        yc = _sc_call(sc, tri_suffix)
        x = _out_call(x, l, mod4, ya, yb, yc, p)
    return x
```

```python
import functools

import jax
import jax.numpy as jnp
from jax import lax
from jax.experimental import pallas as pl
from jax.experimental.pallas import tpu as pltpu

CHUNK = 128
GA_GROUPS = 4
GA_HEAD = 64
GA_WIDTH = GA_GROUPS * GA_HEAD
MB_HEADS = 4
MB_HEAD = 96
MB_PAD = 128
MB_WIDTH = MB_HEADS * MB_HEAD
MB_PWIDTH = MB_HEADS * MB_PAD
MB_CONV = 4
SC_HEADS = 6
SC_HEAD = 64
SC_WIDTH = SC_HEADS * SC_HEAD
GATE_W = 128
EPS = 1e-6
LANE = 128
SC_TILE = 256
VMEM_LIMIT = 56 * 1024 * 1024

F32 = jnp.float32
BF16 = jnp.bfloat16
HIGHEST = lax.Precision.HIGHEST


def _dot(a, b):
    return jnp.dot(a, b, preferred_element_type=F32)


def _dot_nt(a, b):
    return lax.dot_general(a, b, (((1,), (1,)), ((), ())), preferred_element_type=F32)


def _sigmoid(x):
    return 1.0 / (1.0 + jnp.exp(-x))


def _silu(x):
    return x * _sigmoid(x)


def _gelu_tanh(x):
    return 0.5 * x * (1.0 + jnp.tanh(0.7978845608028654 * (x + 0.044715 * (x * x * x))))


def _params(n_axes):
    return pltpu.CompilerParams(
        dimension_semantics=("arbitrary",) * n_axes, vmem_limit_bytes=VMEM_LIMIT)


def _ada_kernel(c_ref, w_ref, b_ref, o_ref):
    c = c_ref[...]
    o_ref[0] = jnp.dot(_silu(c), w_ref[0], precision=HIGHEST,
                       preferred_element_type=F32) + b_ref[0]


def _ada_call(c, w_ada, b_ada):
    depth, d, d3 = w_ada.shape
    b = c.shape[0]
    tn = 1024 if d3 % 1024 == 0 else d3
    return pl.pallas_call(
        _ada_kernel,
        grid=(depth, d3 // tn),
        in_specs=[
            pl.BlockSpec((b, d), lambda l, n: (0, 0)),
            pl.BlockSpec((1, d, tn), lambda l, n: (l, 0, n)),
            pl.BlockSpec((1, 1, tn), lambda l, n: (l, 0, n)),
        ],
        out_specs=pl.BlockSpec((1, b, tn), lambda l, n: (l, 0, n)),
        out_shape=jax.ShapeDtypeStruct((depth, b, d3), F32),
        compiler_params=_params(2),
        name="ada_mod",
    )(c, w_ada, b_ada.reshape(depth, 1, d3))


N_GA = 3 * GA_WIDTH
N_MB = 5 * MB_PWIDTH
N_SC = 4 * SC_WIDTH
OFF_MB = N_GA
OFF_GATE = OFF_MB + N_MB
OFF_SC = OFF_GATE + GATE_W
N_PROJ = OFF_SC + N_SC
NORM_BLOCK = 256


def _inproj_kernel(x_ref, mod_ref, g_ref, w_ref, qkg_ref, gm_ref, vg_ref, ws_ref, bs_ref, gmga_ref,
                   ya_ref, mb_ref, gate_ref, sc_ref, *, d_model, tm):
    x = x_ref[0]
    ms = jnp.mean(x * x, axis=-1, keepdims=True)
    shift = mod_ref[0, :, 0:d_model]
    scale = mod_ref[0, :, d_model:2 * d_model]
    gain = g_ref[...] * (1.0 + scale)
    hb = (x * lax.rsqrt(ms + EPS) * gain + shift).astype(BF16)

    def proj(c0, n):
        return _dot(hb, w_ref[:, c0:c0 + n])

    for c0 in range(0, N_MB, MB_PWIDTH):
        mb_ref[0, :, c0:c0 + MB_PWIDTH] = proj(OFF_MB + c0, MB_PWIDTH).astype(BF16)
    gate_ref[0] = proj(OFF_GATE, GATE_W)

    qk = proj(OFF_SC, 2 * SC_WIDTH)
    sq = (qk * qk).astype(BF16)
    msq = jnp.concatenate([_dot(sq[:, c0:c0 + NORM_BLOCK], gm_ref[...])
                           for c0 in range(0, 2 * SC_WIDTH, NORM_BLOCK)], axis=1)
    sc_ref[0, :, 0:2 * SC_WIDTH] = (qk * lax.rsqrt(msq + EPS) * qkg_ref[...]).astype(BF16)
    sc_ref[0, :, 2 * SC_WIDTH:N_SC] = proj(OFF_SC + 2 * SC_WIDTH, 2 * SC_WIDTH).astype(BF16)

    u_all = _gelu_tanh(proj(0, GA_WIDTH))
    v_all = _gelu_tanh(proj(GA_WIDTH, GA_WIDTH))
    z_all = proj(2 * GA_WIDTH, GA_WIDTH)
    t_idx = lax.broadcasted_iota(jnp.int32, (CHUNK, CHUNK), 0)
    s_idx = lax.broadcasted_iota(jnp.int32, (CHUNK, CHUNK), 1)
    lane = lax.broadcasted_iota(jnp.int32, (CHUNK, GA_WIDTH), 1)
    w_causal = [jnp.where(s_idx <= t_idx, ws_ref[g], 0.0).astype(BF16) for g in range(GA_GROUPS)]
    for c in range(tm // CHUNK):
        rows = slice(c * CHUNK, (c + 1) * CHUNK)
        v = v_all[rows]
        msq_v = _dot((v * v).astype(BF16), gmga_ref[...])
        vn = (v * lax.rsqrt(msq_v + EPS) * vg_ref[...]).astype(BF16)
        sp = bs_ref[...]
        for g in range(GA_GROUPS):
            in_group = (lane >= g * GA_HEAD) & (lane < (g + 1) * GA_HEAD)
            sp = sp + _dot(w_causal[g], jnp.where(in_group, vn, jnp.zeros_like(vn)))
        ya_ref[0, rows, :] = (u_all[rows] * sp * _silu(z_all[rows])).astype(BF16)


def _inproj_call(x, l, mod4, p, gmat, ws, gm_ga):
    b, s, d = x.shape
    tm = 512 if s % 512 == 0 else s
    assert tm % CHUNK == 0
    row = lambda bi, si: (bi, si, 0)
    const2 = lambda bi, si: (0, 0)
    layer2 = lambda bi, si: (l, 0, 0)
    return pl.pallas_call(
        functools.partial(_inproj_kernel, d_model=d, tm=tm),
        grid=(b, s // tm),
        in_specs=[
            pl.BlockSpec((1, tm, d), row),
            pl.BlockSpec((None, 1, 1, 3 * d), lambda bi, si: (l, bi, 0, 0)),
            pl.BlockSpec((None, 1, d), layer2),
            pl.BlockSpec((None, d, N_PROJ), layer2),
            pl.BlockSpec((None, 1, 2 * SC_WIDTH), layer2),
            pl.BlockSpec((NORM_BLOCK, NORM_BLOCK), const2),
            pl.BlockSpec((None, 1, GA_WIDTH), layer2),
            pl.BlockSpec((None, GA_GROUPS, CHUNK, CHUNK), lambda bi, si: (l, 0, 0, 0)),
            pl.BlockSpec((None, CHUNK, GA_WIDTH), layer2),
            pl.BlockSpec((GA_WIDTH, GA_WIDTH), const2),
        ],
        out_specs=[
            pl.BlockSpec((1, tm, GA_WIDTH), row),
            pl.BlockSpec((1, tm, N_MB), row),
            pl.BlockSpec((1, tm, GATE_W), row),
            pl.BlockSpec((1, tm, N_SC), row),
        ],
        out_shape=[
            jax.ShapeDtypeStruct((b, s, GA_WIDTH), BF16),
            jax.ShapeDtypeStruct((b, s, N_MB), BF16),
            jax.ShapeDtypeStruct((b, s, GATE_W), F32),
            jax.ShapeDtypeStruct((b, s, N_SC), BF16),
        ],
        compiler_params=_params(2),
        name="in_proj",
    )(x, mod4, p["norm_g"], p["w_r"], p["qkg"], gmat, p["vg"], ws, p["bs_full"], gm_ga)


MB_NB = 4


def _mb_kernel(q_ref, k_ref, v_ref, o_ref, z_ref, gate_ref, cw_ref, cb_ref, gb_ref,
               hn_ref, tri_ref, y_ref, c_state, m_state, ext_ref, *, nb):
    ci = pl.program_id(1)

    @pl.when(ci == 0)
    def _():
        c_state[...] = jnp.zeros_like(c_state)
        m_state[...] = jnp.zeros_like(m_state)
        ext_ref[:, 0:8, :] = jnp.zeros((nb, 8, 2 * MB_PWIDTH), F32)

    row = lax.broadcasted_iota(jnp.int32, (CHUNK, CHUNK), 0)
    col = lax.broadcasted_iota(jnp.int32, (CHUNK, CHUNK), 1)
    causal_t = row <= col

    for bi in range(nb):
        qk_raw = jnp.concatenate([q_ref[bi], k_ref[bi]], axis=1).astype(F32)
        ext_ref[bi, 8:8 + CHUNK, :] = qk_raw
        conv = cb_ref[...] + cw_ref[MB_CONV - 1:MB_CONV, :] * qk_raw
        for back in range(1, MB_CONV):
            tap = MB_CONV - 1 - back
            conv = conv + cw_ref[tap:tap + 1, :] * ext_ref[bi, 8 - back:8 - back + CHUNK, :]
        ext_ref[bi, 0:8, :] = qk_raw[CHUNK - 8:CHUNK, :]
        qk = _silu(conv)
        q_all = qk[:, 0:MB_PWIDTH].astype(BF16)
        k_all = (qk[:, MB_PWIDTH:2 * MB_PWIDTH] * (MB_HEAD ** -0.5)).astype(BF16)

        graw = gate_ref[bi] + gb_ref[...]
        lane_g = lax.broadcasted_iota(jnp.int32, graw.shape, 1)
        log_f = jnp.minimum(graw, 0.0) - jnp.log1p(jnp.exp(-jnp.abs(graw)))
        gm = jnp.where((lane_g >= MB_HEADS) & (lane_g < 2 * MB_HEADS), log_f, graw)
        bc = jnp.dot(tri_ref[...], gm, precision=HIGHEST, preferred_element_type=F32)
        gm_t = gm.T
        bc_t = bc.T

        for h in range(MB_HEADS):
            cs = slice(h * MB_PAD, (h + 1) * MB_PAD)
            b_row = bc_t[MB_HEADS + h:MB_HEADS + h + 1, :]
            a_row = gm_t[h:h + 1, :] - b_row
            a_col = gm[:, h:h + 1] - bc[:, MB_HEADS + h:MB_HEADS + h + 1]
            m_prev = m_state[bi, h:h + 1, :]

            d_log = jnp.where(causal_t, b_row + a_col, -jnp.inf)
            inter = b_row + m_prev
            m_t = jnp.maximum(inter, jnp.max(d_log, axis=0, keepdims=True))
            w_intra = jnp.exp(d_log - m_t)
            w_inter = jnp.exp(inter - m_t)

            qh = q_all[:, cs]
            kh = k_all[:, cs]
            v_t = v_ref[bi, :, cs].astype(F32).T
            v_aug = jnp.where(row == MB_HEAD, 1.0, v_t)
            s_mat = _dot_nt(kh, qh) * w_intra
            num = _dot(v_aug.astype(BF16), s_mat.astype(BF16)) \
                + w_inter * _dot_nt(c_state[bi, h].astype(BF16), qh)
            den = num[MB_HEAD:MB_HEAD + 1, :]
            hh = num * (1.0 / jnp.maximum(jnp.abs(den), jnp.exp(-m_t)))
            hh = jnp.where(row < MB_HEAD, hh, 0.0)

            b_tot = b_row[:, CHUNK - 1:CHUNK]
            m_new = m_t[:, CHUNK - 1:CHUNK]
            decay = jnp.exp(b_tot + m_prev[:, 0:1] - m_new)
            ws_row = jnp.exp(b_tot + a_row - m_new)
            c_state[bi, h] = decay * c_state[bi, h] + _dot((v_aug * ws_row).astype(BF16), kh)
            m_state[bi, h:h + 1, :] = jnp.broadcast_to(m_new, (1, LANE))

            msq = jnp.sum(hh * hh, axis=0, keepdims=True) * (1.0 / MB_HEAD)
            hn = (hh * lax.rsqrt(msq + EPS) * hn_ref[h]).T
            y = _sigmoid(o_ref[bi, :, cs].astype(F32)) * hn * _silu(z_ref[bi, :, cs].astype(F32))
            y_ref[bi, :, cs] = y.astype(BF16)


def _mb_call(mb, gates, l, p, tri):
    b, s, _ = mb.shape
    nc = s // CHUNK
    nb = MB_NB if b % MB_NB == 0 else 1
    col = lambda j: (lambda bi, ci: (bi, ci, j))
    const2 = lambda bi, ci: (0, 0)
    layer2 = lambda bi, ci: (l, 0, 0)
    return pl.pallas_call(
        functools.partial(_mb_kernel, nb=nb),
        grid=(b // nb, nc),
        in_specs=[
            pl.BlockSpec((nb, CHUNK, MB_PWIDTH), col(0)),
            pl.BlockSpec((nb, CHUNK, MB_PWIDTH), col(1)),
            pl.BlockSpec((nb, CHUNK, MB_PWIDTH), col(2)),
            pl.BlockSpec((nb, CHUNK, MB_PWIDTH), col(3)),
            pl.BlockSpec((nb, CHUNK, MB_PWIDTH), col(4)),
            pl.BlockSpec((nb, CHUNK, GATE_W), col(0)),
            pl.BlockSpec((None, MB_CONV, 2 * MB_PWIDTH), layer2),
            pl.BlockSpec((None, 1, 2 * MB_PWIDTH), layer2),
            pl.BlockSpec((None, 1, GATE_W), layer2),
            pl.BlockSpec((None, MB_HEADS, MB_PAD, CHUNK), lambda bi, ci: (l, 0, 0, 0)),
            pl.BlockSpec((CHUNK, CHUNK), const2),
        ],
        out_specs=pl.BlockSpec((nb, CHUNK, MB_PWIDTH), col(0)),
        out_shape=jax.ShapeDtypeStruct((b, s, MB_PWIDTH), BF16),
        scratch_shapes=[
            pltpu.VMEM((nb, MB_HEADS, MB_PAD, MB_PAD), F32),
            pltpu.VMEM((nb, 8, LANE), F32),
            pltpu.VMEM((nb, 8 + CHUNK, 2 * MB_PWIDTH), F32),
        ],
        compiler_params=_params(2),
        name="mlstm",
    )(mb, mb, mb, mb, mb, gates, p["conv_w"], p["conv_b"], p["gate_b"], p["hnorm"], tri)


LOG2E = 1.4426950408889634
SOFTPLUS_CAP = 64.0
SC_QTILE = 1024


def _sc_kernel(q_ref, k_ref, v_ref, z_ref, tri_ref, y_ref, qm_ref, acc_ref, o_ref, zs_ref, cs_ref,
               sp_ref, wb_ref, *, tile, nsub):
    qi = pl.program_id(1)
    npair = SC_WIDTH // LANE
    nch = nsub * SC_HEADS
    low = lax.broadcasted_iota(jnp.int32, (tile, LANE), 1) < SC_HEAD
    for u in range(nsub):
        for p in range(npair):
            q = q_ref[0, u * tile:(u + 1) * tile, p * LANE:(p + 1) * LANE]
            zero = jnp.zeros_like(q)
            qm_ref[u * SC_HEADS + 2 * p] = jnp.where(low, q, zero)
            qm_ref[u * SC_HEADS + 2 * p + 1] = jnp.where(low, zero, q)
    acc_ref[...] = jnp.zeros_like(acc_ref)
    o_ref[...] = jnp.zeros_like(o_ref)

    def chains_of(p, modes):
        ch = [(u, hh) for u in range(nsub) if modes[u] is not None for hh in range(2)]
        return ch, [u * SC_HEADS + 2 * p + hh for (u, hh) in ch]

    def strict_mask(modes):
        if 'diag' not in modes:
            return None
        t_idx = lax.broadcasted_iota(jnp.int32, (tile, tile), 0)
        s_idx = lax.broadcasted_iota(jnp.int32, (tile, tile), 1)
        return s_idx < t_idx

    def stage_logits(j, modes, slot):
        start = pl.multiple_of(j * tile, tile)
        strict = strict_mask(modes)
        for p in range(npair):
            kt = k_ref[0, pl.ds(start, tile), p * LANE:(p + 1) * LANE]
            chains, idx = chains_of(p, modes)
            for c in idx:
                zs_ref[slot * nch + c] = _dot_nt(qm_ref[c], kt)
            for c, (u, hh) in zip(idx, chains):
                z = zs_ref[slot * nch + c]
                sp = jnp.maximum(z, jnp.log(1.0 + jnp.exp2(jnp.minimum(z, SOFTPLUS_CAP))) * LOG2E)
                if modes[u] == 'diag':
                    sp = jnp.where(strict, sp, 0.0)
                sp_ref[slot * nch + c] = sp.astype(BF16)

    def stage_weights(j, modes, slot):
        start = pl.multiple_of(j * tile, tile)
        strict = strict_mask(modes)
        for p in range(npair):
            cols = slice(p * LANE, (p + 1) * LANE)
            vt = v_ref[0, pl.ds(start, tile), cols]
            zero_v = jnp.zeros_like(vt)
            vm = (jnp.where(low, vt, zero_v), jnp.where(low, zero_v, vt))
            chains, idx = chains_of(p, modes)
            for c in idx:
                cs_ref[c] = _dot(sp_ref[slot * nch + c], tri_ref[...])
            for c, (u, hh) in zip(idx, chains):
                cs = cs_ref[c]
                acc = acc_ref[c]
                w = jnp.exp2(zs_ref[slot * nch + c] - cs - jnp.concatenate([acc] * (tile // LANE), axis=1))
                if modes[u] == 'diag':
                    w = jnp.where(strict, w, 0.0)
                acc_ref[c] = acc + jnp.broadcast_to(cs[:, 0:1], acc.shape)
                wb_ref[c] = w.astype(BF16)
            for c, (u, hh) in zip(idx, chains):
                o_ref[u * tile:(u + 1) * tile, cols] += _dot(wb_ref[c], vm[hh])

    for c in reversed(range(nsub)):
        modes = [None if u < c else ('diag' if u == c else 'full') for u in range(nsub)]
        stage_logits(qi * nsub + c, modes, 0)
        stage_weights(qi * nsub + c, modes, 0)

    n = qi * nsub
    full = ['full'] * nsub

    @pl.when(n > 0)
    def _():
        stage_logits(n - 1, full, 0)

    def body(i, carry):
        stage_weights(n - i, full, (i - 1) % 2)
        stage_logits(n - 1 - i, full, i % 2)
        return carry

    lax.fori_loop(1, n, body, 0)

    @pl.when(n > 0)
    def _():
        stage_weights(0, full, (n - 1) % 2)

    y_ref[0] = (o_ref[...] * _silu(z_ref[0].astype(F32))).astype(BF16)


def _sc_call(sc, tri):
    b, s, _ = sc.shape
    tile = SC_TILE
    tq = SC_QTILE if s % SC_QTILE == 0 else tile
    nsub = tq // tile
    nch = nsub * SC_HEADS
    return pl.pallas_call(
        functools.partial(_sc_kernel, tile=tile, nsub=nsub),
        grid=(b, s // tq),
        in_specs=[
            pl.BlockSpec((1, tq, SC_WIDTH), lambda bi, qi: (bi, qi, 0)),
            pl.BlockSpec((1, s, SC_WIDTH), lambda bi, qi: (bi, 0, 1)),
            pl.BlockSpec((1, s, SC_WIDTH), lambda bi, qi: (bi, 0, 2)),
            pl.BlockSpec((1, tq, SC_WIDTH), lambda bi, qi: (bi, qi, 3)),
            pl.BlockSpec((tile, tile), lambda bi, qi: (0, 0)),
        ],
        out_specs=pl.BlockSpec((1, tq, SC_WIDTH), lambda bi, qi: (bi, qi, 0)),
        out_shape=jax.ShapeDtypeStruct((b, s, SC_WIDTH), BF16),
        scratch_shapes=[
            pltpu.VMEM((nch, tile, LANE), BF16),
            pltpu.VMEM((nch, tile, LANE), F32),
            pltpu.VMEM((tq, SC_WIDTH), F32),
            pltpu.VMEM((2 * nch, tile, tile), F32),
            pltpu.VMEM((nch, tile, tile), F32),
            pltpu.VMEM((2 * nch, tile, tile), BF16),
            pltpu.VMEM((nch, tile, tile), BF16),
        ],
        compiler_params=_params(2),
        name="stick_breaking",
    )(sc, sc, sc, sc, tri)


def _out_kernel(x_ref, mod_ref, ya_ref, yb_ref, yc_ref, wa_ref, wb_ref, wc_ref, o_ref,
                *, d_model):
    y = _dot(ya_ref[0], wa_ref[...]) + _dot(yb_ref[0], wb_ref[...]) + _dot(yc_ref[0], wc_ref[...])
    gate = mod_ref[0, :, 2 * d_model:3 * d_model]
    o_ref[0] = x_ref[0] + gate * y


def _out_call(x, l, mod4, ya, yb, yc, p):
    b, s, d = x.shape
    tm = 1024 if s % 1024 == 0 else s
    row = lambda bi, si: (bi, si, 0)
    layer2 = lambda bi, si: (l, 0, 0)
    return pl.pallas_call(
        functools.partial(_out_kernel, d_model=d),
        grid=(b, s // tm),
        in_specs=[
            pl.BlockSpec((1, tm, d), row),
            pl.BlockSpec((None, 1, 1, 3 * d), lambda bi, si: (l, bi, 0, 0)),
            pl.BlockSpec((1, tm, GA_WIDTH), row),
            pl.BlockSpec((1, tm, MB_PWIDTH), row),
            pl.BlockSpec((1, tm, SC_WIDTH), row),
            pl.BlockSpec((None, GA_WIDTH, d), layer2),
            pl.BlockSpec((None, MB_PWIDTH, d), layer2),
            pl.BlockSpec((None, SC_WIDTH, d), layer2),
        ],
        out_specs=pl.BlockSpec((1, tm, d), row),
        out_shape=jax.ShapeDtypeStruct((b, s, d), F32),
        compiler_params=_params(2),
        name="out_proj",
    )(x, mod4, ya, yb, yc, p["wa"], p["wb"], p["wc"])


def _pad_heads(a):
    lead = a.shape[:-1]
    a = a.reshape(lead + (MB_HEADS, MB_HEAD))
    a = jnp.pad(a, [(0, 0)] * len(lead) + [(0, 0), (0, MB_PAD - MB_HEAD)])
    return a.reshape(lead + (MB_PWIDTH,))


def _block_mean_matrix(width, group):
    idx = jnp.arange(width) // group
    return jnp.where(idx[:, None] == idx[None, :], 1.0 / group, 0.0).astype(BF16)


def _stacked_params(norm_g, w_in, ga_v_norm, ga_bs, mb_conv_w, mb_conv_b, mb_b_i, mb_b_f, mb_h_norm,
                    sc_q_norm, sc_k_norm, w_out):
    depth, d, _ = w_in.shape
    o_mb = N_GA
    o_gate = o_mb + 5 * MB_WIDTH
    o_sc = o_gate + 2 * MB_HEADS
    w_mb = _pad_heads(w_in[:, :, o_mb:o_gate].reshape(depth, d, 5, MB_WIDTH)).reshape(depth, d, N_MB)
    w_gate = jnp.pad(w_in[:, :, o_gate:o_sc], ((0, 0), (0, 0), (0, GATE_W - 2 * MB_HEADS)))
    w_r = jnp.concatenate([w_in[:, :, :N_GA], w_mb, w_gate, w_in[:, :, o_sc:]], axis=2).astype(BF16)

    conv_w = _pad_heads(mb_conv_w.reshape(depth, MB_CONV, 2, MB_WIDTH)).reshape(depth, MB_CONV, 2 * MB_PWIDTH)
    conv_b = _pad_heads(mb_conv_b.reshape(depth, 1, 2, MB_WIDTH)).reshape(depth, 1, 2 * MB_PWIDTH)
    gate_b = jnp.pad(jnp.concatenate([mb_b_i, mb_b_f], axis=1), ((0, 0), (0, GATE_W - 2 * MB_HEADS)))[:, None, :]
    hnorm = jnp.pad(mb_h_norm, ((0, 0), (0, 0), (0, MB_PAD - MB_HEAD)))
    hnorm = jnp.broadcast_to(hnorm[:, :, :, None], (depth, MB_HEADS, MB_PAD, CHUNK))

    qkg = jnp.concatenate([jnp.tile(sc_q_norm, (1, SC_HEADS)) * (SC_HEAD ** -0.5 * LOG2E),
                           jnp.tile(sc_k_norm, (1, SC_HEADS))], axis=1)[:, None, :]
    vg = ga_v_norm.reshape(depth, 1, GA_WIDTH)
    bs_full = jnp.repeat(jnp.swapaxes(ga_bs, 1, 2), GA_HEAD, axis=2)

    wa = w_out[:, :GA_WIDTH].astype(BF16)
    wb = w_out[:, GA_WIDTH:GA_WIDTH + MB_WIDTH].reshape(depth, MB_HEADS, MB_HEAD, d)
    wb = jnp.pad(wb, ((0, 0), (0, 0), (0, MB_PAD - MB_HEAD), (0, 0))).reshape(depth, MB_PWIDTH, d).astype(BF16)
    wc = w_out[:, GA_WIDTH + MB_WIDTH:].astype(BF16)
    return dict(norm_g=norm_g[:, None, :], w_r=w_r, conv_w=conv_w, conv_b=conv_b, gate_b=gate_b, hnorm=hnorm,
                qkg=qkg, vg=vg, bs_full=bs_full, wa=wa, wb=wb, wc=wc)


def kernel(x, c, norm_g, w_ada, b_ada, w_in, ga_v_norm, ga_ws, ga_bs, mb_conv_w, mb_conv_b,
           mb_b_i, mb_b_f, mb_h_norm, sc_q_norm, sc_k_norm, w_out):
    depth = w_in.shape[0]
    b, s, d = x.shape
    assert s % SC_TILE == 0 and s % CHUNK == 0 and d % LANE == 0

    mod4 = _ada_call(c, w_ada, b_ada)[:, :, None, :]
    p = _stacked_params(norm_g, w_in, ga_v_norm, ga_bs, mb_conv_w, mb_conv_b, mb_b_i, mb_b_f, mb_h_norm,
                        sc_q_norm, sc_k_norm, w_out)
    gm_sc = _block_mean_matrix(NORM_BLOCK, SC_HEAD)
    gm_ga = _block_mean_matrix(GA_WIDTH, GA_HEAD)
    r = jnp.arange(CHUNK)
    tri_low = (r[None, :] <= r[:, None]).astype(F32)
    r2 = jnp.arange(SC_TILE)
    tri_suffix = (r2[:, None] >= r2[None, :]).astype(BF16)

    for l in range(depth):
        ya, mb, gates, sc = _inproj_call(x, l, mod4, p, gm_sc, ga_ws, gm_ga)
        yb = _mb_call(mb, gates, l, p, tri_low)
        yc = _sc_call(sc, tri_suffix)
        x = _out_call(x, l, mod4, ya, yb, yc, p)
    return x
```
